```python
import jax, jax.numpy as jnp
from jax import lax
import numpy as np

D_MODEL = 1024
BATCH = 8
SEQ = 8192
DEPTH = 1

CHUNK = 64
SGU_CHUNK = 128
SGU_WIDTH = D_MODEL // 2
SGU_GROUPS = 4
SGU_GROUP_DIM = SGU_WIDTH // SGU_GROUPS
MLSTM_WIDTH = D_MODEL // 2
MLSTM_HEADS = 4
MLSTM_V_DIM = MLSTM_WIDTH // MLSTM_HEADS
MLSTM_QK_DIM = MLSTM_V_DIM // 2
QK_TOTAL = MLSTM_HEADS * MLSTM_QK_DIM
CONV_WIDTH = 4
MIX_WIDTH = SGU_WIDTH + MLSTM_WIDTH
IN_WIDTH = 2 * SGU_WIDTH + 2 * QK_TOTAL + 2 * MLSTM_WIDTH + 2 * MLSTM_HEADS
D_FF = ((8 * D_MODEL // 3 + 127) // 128) * 128
EPS = 1e-6

kernel_name = "hybrid_sgu_mlstm_macaron_block"


def _rmsnorm(x, g):
    x32 = x.astype(jnp.float32)
    y = x32 * lax.rsqrt(jnp.mean(x32 * x32, axis=-1, keepdims=True) + EPS)
    return (y * g.astype(jnp.float32)).astype(x.dtype)


def _layernorm(x, g, b):
    x32 = x.astype(jnp.float32)
    mu = jnp.mean(x32, axis=-1, keepdims=True)
    xc = x32 - mu
    y = xc * lax.rsqrt(jnp.mean(xc * xc, axis=-1, keepdims=True) + EPS)
    return (y * g.astype(jnp.float32) + b.astype(jnp.float32)).astype(x.dtype)


def _swiglu(x, w_gate, w_up, w_down):
    return (jax.nn.silu(x @ w_gate) * (x @ w_up)) @ w_down


def _spatial_gating(z, ln_g, ln_b, w_s, b_s):
    bsz, seq, _ = z.shape
    u, v = z[..., :SGU_WIDTH], z[..., SGU_WIDTH:]
    v = _layernorm(v, ln_g, ln_b)
    vg = v.reshape(bsz, seq // SGU_CHUNK, SGU_CHUNK, SGU_GROUPS, SGU_GROUP_DIM)
    blk = jnp.arange(SGU_CHUNK) // CHUNK
    mask = blk[:, None] >= blk[None, :]
    w = jnp.where(mask[None], w_s, jnp.zeros_like(w_s))
    mixed = jnp.einsum('gts,bcsgd->bctgd', w, vg) + b_s.T[:, :, None]
    return u * mixed.reshape(bsz, seq, SGU_WIDTH)


def _causal_depthwise_conv(x, w, b):
    c = x.shape[-1]
    y = lax.conv_general_dilated(
        x, w[:, None, :], window_strides=(1,), padding=[(CONV_WIDTH - 1, 0)],
        dimension_numbers=('NWC', 'WIO', 'NWC'), feature_group_count=c)
    return y + b


def _mlstm_chunkwise(q, k, v, i_pre, f_pre):
    bsz, nh, seq, dk = q.shape
    dv = v.shape[-1]
    nc = seq // CHUNK
    q = q.astype(jnp.float32).reshape(bsz, nh, nc, CHUNK, dk)
    k = k.astype(jnp.float32).reshape(bsz, nh, nc, CHUNK, dk)
    v = v.astype(jnp.float32).reshape(bsz, nh, nc, CHUNK, dv)
    ig = i_pre.astype(jnp.float32).reshape(bsz, nh, nc, CHUNK)
    logf = jax.nn.log_sigmoid(f_pre.astype(jnp.float32)).reshape(bsz, nh, nc, CHUNK)
    b = jnp.cumsum(logf, axis=-1)
    b_last = b[..., -1]

    g = b_last[..., None] - b + ig
    m_loc = jnp.max(g, axis=-1)
    wg = jnp.exp(g - m_loc[..., None])
    c_loc = jnp.einsum('bhcl,bhclv,bhclk->bhcvk', wg, v, k)
    n_loc = jnp.einsum('bhcl,bhclk->bhck', wg, k)

    def step(carry, xs):
        c_st, n_st, m_st = carry
        c_l, n_l, m_l, bl = xs
        m_new = jnp.maximum(bl + m_st, m_l)
        a = jnp.exp(bl + m_st - m_new)
        e = jnp.exp(m_l - m_new)
        c_new = a[..., None, None] * c_st + e[..., None, None] * c_l
        n_new = a[..., None] * n_st + e[..., None] * n_l
        return (c_new, n_new, m_new), (c_st, n_st, m_st)

    init = (jnp.zeros((bsz, nh, dv, dk), jnp.float32),
            jnp.zeros((bsz, nh, dk), jnp.float32),
            jnp.zeros((bsz, nh), jnp.float32))
    xs = (jnp.moveaxis(c_loc, 2, 0), jnp.moveaxis(n_loc, 2, 0),
          jnp.moveaxis(m_loc, 2, 0), jnp.moveaxis(b_last, 2, 0))
    _, (c_prev, n_prev, m_prev) = lax.scan(step, init, xs)
    c_prev = jnp.moveaxis(c_prev, 0, 2)
    n_prev = jnp.moveaxis(n_prev, 0, 2)
    m_prev = jnp.moveaxis(m_prev, 0, 2)

    d = b[..., :, None] - b[..., None, :] + ig[..., None, :]
    tri = jnp.tril(jnp.ones((CHUNK, CHUNK), dtype=bool))
    d = jnp.where(tri, d, -jnp.inf)
    inter_log = b + m_prev[..., None]
    m_t = jnp.maximum(inter_log, jnp.max(d, axis=-1))
    w = jnp.exp(d - m_t[..., None]) * jnp.einsum('bhctd,bhcsd->bhcts', q, k)
    a_inter = jnp.exp(inter_log - m_t)
    num = (jnp.einsum('bhcts,bhcsv->bhctv', w, v)
           + a_inter[..., None] * jnp.einsum('bhcvk,bhctk->bhctv', c_prev, q))
    den = jnp.sum(w, axis=-1) + a_inter * jnp.einsum('bhck,bhctk->bhct', n_prev, q)
    h = num / jnp.maximum(jnp.abs(den), jnp.exp(-m_t))[..., None]
    return h.reshape(bsz, nh, seq, dv)


def _mlstm_mixer(zb, conv_w, conv_b, igate_b, fgate_b, mh_norm):
    bsz, seq, _ = zb.shape
    o0 = 2 * QK_TOTAL
    o1 = o0 + MLSTM_WIDTH
    o2 = o1 + MLSTM_WIDTH
    o3 = o2 + MLSTM_HEADS
    qk = jax.nn.silu(_causal_depthwise_conv(zb[..., :o0], conv_w, conv_b))
    v_raw = zb[..., o0:o1]
    o_raw = zb[..., o1:o2]
    i_raw = zb[..., o2:o3] + igate_b
    f_raw = zb[..., o3:] + fgate_b
    q = qk[..., :QK_TOTAL].reshape(bsz, seq, MLSTM_HEADS, MLSTM_QK_DIM).transpose(0, 2, 1, 3)
    k = (qk[..., QK_TOTAL:] * (MLSTM_QK_DIM ** -0.5)).reshape(
        bsz, seq, MLSTM_HEADS, MLSTM_QK_DIM).transpose(0, 2, 1, 3)
    v = v_raw.reshape(bsz, seq, MLSTM_HEADS, MLSTM_V_DIM).transpose(0, 2, 1, 3)
    h = _mlstm_chunkwise(q, k, v, i_raw.transpose(0, 2, 1), f_raw.transpose(0, 2, 1))
    h = h * lax.rsqrt(jnp.mean(h * h, axis=-1, keepdims=True) + EPS)
    h = h.transpose(0, 2, 1, 3).reshape(bsz, seq, MLSTM_WIDTH) * mh_norm.astype(jnp.float32)
    return (h * jax.nn.sigmoid(o_raw.astype(jnp.float32))).astype(zb.dtype)


def setup_inputs(seed: int = 0) -> dict:
    key = jax.random.key(seed)
    ks = jax.random.split(key, 24)

    def nrm(k, shape, scale):
        return jax.random.normal(k, shape, jnp.float32) * scale

    return {
        "x": nrm(ks[0], (BATCH, SEQ, D_MODEL), 1.0),
        "ffn1_norm": 1.0 + nrm(ks[1], (DEPTH, D_MODEL), 0.02),
        "ffn1_w_gate": nrm(ks[2], (DEPTH, D_MODEL, D_FF), D_MODEL ** -0.5),
        "ffn1_w_up": nrm(ks[3], (DEPTH, D_MODEL, D_FF), D_MODEL ** -0.5),
        "ffn1_w_down": nrm(ks[4], (DEPTH, D_FF, D_MODEL), D_FF ** -0.5),
        "mix_norm": 1.0 + nrm(ks[5], (DEPTH, D_MODEL), 0.02),
        "w_in": nrm(ks[6], (DEPTH, D_MODEL, IN_WIDTH), D_MODEL ** -0.5),
        "sgu_ln_g": 1.0 + nrm(ks[7], (DEPTH, SGU_WIDTH), 0.02),
        "sgu_ln_b": nrm(ks[8], (DEPTH, SGU_WIDTH), 0.02),
        "sgu_w": nrm(ks[9], (DEPTH, SGU_GROUPS, SGU_CHUNK, SGU_CHUNK), SGU_CHUNK ** -0.5),
        "sgu_b": 1.0 + nrm(ks[10], (DEPTH, SGU_GROUPS, SGU_CHUNK), 0.1),
        "conv_w": nrm(ks[11], (DEPTH, CONV_WIDTH, 2 * QK_TOTAL), CONV_WIDTH ** -0.5),
        "conv_b": nrm(ks[12], (DEPTH, 2 * QK_TOTAL), 0.02),
        "igate_b": nrm(ks[13], (DEPTH, MLSTM_HEADS), 0.1),
        "fgate_b": jnp.linspace(3.0, 6.0, MLSTM_HEADS, dtype=jnp.float32)[None, :]
                    + nrm(ks[14], (DEPTH, MLSTM_HEADS), 0.1),
        "mh_norm": 1.0 + nrm(ks[15], (DEPTH, MLSTM_WIDTH), 0.02),
        "w_out": nrm(ks[16], (DEPTH, MIX_WIDTH, D_MODEL), MIX_WIDTH ** -0.5),
        "ffn2_norm": 1.0 + nrm(ks[17], (DEPTH, D_MODEL), 0.02),
        "ffn2_w_gate": nrm(ks[18], (DEPTH, D_MODEL, D_FF), D_MODEL ** -0.5),
        "ffn2_w_up": nrm(ks[19], (DEPTH, D_MODEL, D_FF), D_MODEL ** -0.5),
        "ffn2_w_down": nrm(ks[20], (DEPTH, D_FF, D_MODEL), D_FF ** -0.5),
        "final_norm": 1.0 + nrm(ks[21], (D_MODEL,), 0.02),
    }


def reference(x, ffn1_norm, ffn1_w_gate, ffn1_w_up, ffn1_w_down, mix_norm, w_in,
              sgu_ln_g, sgu_ln_b, sgu_w, sgu_b, conv_w, conv_b, igate_b, fgate_b,
              mh_norm, w_out, ffn2_norm, ffn2_w_gate, ffn2_w_up, ffn2_w_down, final_norm):
    for l in range(DEPTH):
        h = _rmsnorm(x, ffn1_norm[l])
        x = x + 0.5 * _swiglu(h, ffn1_w_gate[l], ffn1_w_up[l], ffn1_w_down[l])
        h = _rmsnorm(x, mix_norm[l])
        z = h @ w_in[l]
        y_a = _spatial_gating(jax.nn.gelu(z[..., :2 * SGU_WIDTH]),
                              sgu_ln_g[l], sgu_ln_b[l], sgu_w[l], sgu_b[l])
        y_b = _mlstm_mixer(z[..., 2 * SGU_WIDTH:], conv_w[l], conv_b[l],
                           igate_b[l], fgate_b[l], mh_norm[l])
        x = x + jnp.concatenate([y_a, y_b], axis=-1) @ w_out[l]
        h = _rmsnorm(x, ffn2_norm[l])
        x = x + 0.5 * _swiglu(h, ffn2_w_gate[l], ffn2_w_up[l], ffn2_w_down[l])
    return _rmsnorm(x, final_norm)
```

```python
import functools

import jax
import jax.numpy as jnp
from jax import lax
from jax.experimental import pallas as pl
from jax.experimental.pallas import tpu as pltpu

EPS = 1e-6
LANES = 128
SUBLANES = 8
VMEM_LIMIT_BYTES = 56 * 1024 * 1024

STREAM_CHUNK = 64
SGU_CHUNK = 128
SGU_GROUPS = 4
HEADS = 4
QK_DIM = 64
V_DIM = 128
MIX_CHUNK = 128

FFN_ROWS = 512
FFN_COLS = 256
MIX_ROWS = 512


def _resident(shape):
    zeros = (0,) * len(shape)
    return pl.BlockSpec(shape, lambda *_: zeros, pipeline_mode=pl.Buffered(1))


def _rms(x, gain):
    return x * lax.rsqrt(jnp.mean(x * x, axis=-1, keepdims=True) + EPS) * gain


def _ffn_kernel(x_ref, gain_ref, wg_ref, wu_ref, wd_ref, fin_ref, o_ref, act_ref, *, final_norm):
    x = x_ref[...]
    h = _rms(x, gain_ref[...]).astype(jnp.bfloat16)
    d_ff = wg_ref.shape[1]
    for c in range(d_ff // FFN_COLS):
        cols = slice(c * FFN_COLS, (c + 1) * FFN_COLS)
        g = jnp.dot(h, wg_ref[:, cols], preferred_element_type=jnp.float32)
        u = jnp.dot(h, wu_ref[:, cols], preferred_element_type=jnp.float32)
        act_ref[:, cols] = (g * jax.nn.sigmoid(g) * u).astype(jnp.bfloat16)
    y = jnp.dot(act_ref[...], wd_ref[...], preferred_element_type=jnp.float32)
    out = x + 0.5 * y
    if final_norm:
        out = _rms(out, fin_ref[...])
    o_ref[...] = out


def _ffn(x2d, gain, wg, wu, wd, fin, final_norm):
    n, d = x2d.shape
    d_ff = wg.shape[1]
    assert n % FFN_ROWS == 0 and d_ff % FFN_COLS == 0
    row_spec = pl.BlockSpec((FFN_ROWS, d), lambda i: (i, 0))
    return pl.pallas_call(
        functools.partial(_ffn_kernel, final_norm=final_norm),
        grid=(n // FFN_ROWS,),
        in_specs=[row_spec, _resident((1, d)), _resident((d, d_ff)), _resident((d, d_ff)),
                  _resident((d_ff, d)), _resident((1, d))],
        out_specs=row_spec,
        out_shape=jax.ShapeDtypeStruct((n, d), jnp.float32),
        scratch_shapes=[pltpu.VMEM((FFN_ROWS, d_ff), jnp.bfloat16)],
        compiler_params=pltpu.CompilerParams(
            dimension_semantics=("arbitrary",), vmem_limit_bytes=VMEM_LIMIT_BYTES),
        name="ffn_final" if final_norm else "ffn",
    )(x2d, gain, wg, wu, wd, fin)


def _scan_lanes(x, combine, fill):
    width = x.shape[1]
    lane = lax.broadcasted_iota(jnp.int32, x.shape, 1)
    d = 1
    while d < width:
        x = combine(x, jnp.where(lane >= d, pltpu.roll(x, d, 1), fill))
        d *= 2
    return x


def _mixer_kernel(x_ref, gain_ref, win_ref, lng_ref, lnb_ref, sw_ref, sb_ref, cw_ref, cb_ref,
                  gb_ref, mh_ref, wout_ref, o_ref,
                  state_ref, mcar_ref, ext_ref, qk_ref, v_ref, og_ref, ymix_ref):
    rows_t = x_ref.shape[0]
    n_chunks = rows_t // MIX_CHUNK
    sgu_w = SGU_GROUPS * LANES
    qk_w = 2 * HEADS * QK_DIM
    mv_w = HEADS * V_DIM
    o_uv, o_qk = 0, 2 * sgu_w
    o_v = o_qk + qk_w
    o_o = o_v + mv_w
    o_g = o_o + mv_w

    @pl.when(pl.program_id(1) == 0)
    def _():
        state_ref[...] = jnp.zeros_like(state_ref)
        mcar_ref[...] = jnp.zeros_like(mcar_ref)
        ext_ref[0:SUBLANES, :] = jnp.zeros((SUBLANES, qk_w), jnp.float32)

    x = x_ref[...]
    h = _rms(x, gain_ref[...]).astype(jnp.bfloat16)

    def proj(lo, width):
        return jnp.dot(h, win_ref[:, lo:lo + width], preferred_element_type=jnp.float32)

    uv = jax.nn.gelu(proj(o_uv, 2 * sgu_w))
    u = uv[:, :sgu_w]
    v = uv[:, sgu_w:]
    vc = v - jnp.mean(v, axis=-1, keepdims=True)
    vn = vc * lax.rsqrt(jnp.mean(vc * vc, axis=-1, keepdims=True) + EPS) * lng_ref[...] + lnb_ref[...]
    vn = vn.astype(jnp.bfloat16)
    t_blk = lax.broadcasted_iota(jnp.int32, (SGU_CHUNK, SGU_CHUNK), 0) // STREAM_CHUNK
    s_blk = lax.broadcasted_iota(jnp.int32, (SGU_CHUNK, SGU_CHUNK), 1) // STREAM_CHUNK
    for g in range(SGU_GROUPS):
        cols = slice(g * LANES, (g + 1) * LANES)
        wm = jnp.where(t_blk >= s_blk, sw_ref[g], 0.0).astype(jnp.bfloat16)
        rhs = jnp.concatenate(
            [vn[c * SGU_CHUNK:(c + 1) * SGU_CHUNK, cols] for c in range(n_chunks)], axis=1)
        mixed = jnp.dot(wm, rhs, preferred_element_type=jnp.float32)
        for c in range(n_chunks):
            rows = slice(c * SGU_CHUNK, (c + 1) * SGU_CHUNK)
            gated = u[rows, cols] * (mixed[:, c * LANES:(c + 1) * LANES] + sb_ref[:, cols])
            ymix_ref[rows, cols] = gated.astype(jnp.bfloat16)

    ext_ref[SUBLANES:, :] = proj(o_qk, qk_w)
    conv = cb_ref[...]
    for j in range(cw_ref.shape[0]):
        conv = conv + cw_ref[j:j + 1, :] * ext_ref[pl.ds(SUBLANES - 3 + j, rows_t), :]
    qk_ref[...] = conv * jax.nn.sigmoid(conv)
    ext_ref[0:SUBLANES, :] = ext_ref[rows_t:rows_t + SUBLANES, :]
    v_ref[...] = proj(o_v, mv_w).astype(jnp.bfloat16)
    og_ref[...] = jax.nn.sigmoid(proj(o_o, mv_w))

    gts = proj(o_g, LANES).T[0:SUBLANES, :] + gb_ref[...]
    logf = pltpu.roll(jax.nn.log_sigmoid(gts), HEADS, 0)
    b_cum = _scan_lanes(logf, jnp.add, 0.0)
    g_in = gts - b_cum
    mm = jnp.maximum(mcar_ref[...], _scan_lanes(g_in, jnp.maximum, -jnp.inf))
    mm_last = jnp.concatenate(
        [jnp.broadcast_to(mm[:, (c + 1) * MIX_CHUNK - 1:(c + 1) * MIX_CHUNK], (SUBLANES, MIX_CHUNK))
         for c in range(n_chunks)], axis=1)
    mm_prev = jnp.concatenate([mcar_ref[:, :MIX_CHUNK], mm_last[:, :rows_t - MIX_CHUNK]], axis=1)
    decay = jnp.exp(mm_prev - mm_last)
    p_last = jnp.exp(g_in - mm_last)
    m_end = b_cum + mm
    mcar_ref[...] = jnp.broadcast_to(m_end[:, rows_t - 1:rows_t], (SUBLANES, rows_t))
    cols_t = jnp.concatenate(
        [-mm, -m_end, jnp.zeros((LANES - 2 * SUBLANES, rows_t), jnp.float32)], axis=0).T

    lane = lax.broadcasted_iota(jnp.int32, (1, LANES), 1)
    t_idx = lax.broadcasted_iota(jnp.int32, (MIX_CHUNK, MIX_CHUNK), 0)
    s_idx = lax.broadcasted_iota(jnp.int32, (MIX_CHUNK, MIX_CHUNK), 1)
    causal = s_idx <= t_idx
    ones_aug = jnp.ones((MIX_CHUNK, V_DIM), jnp.bfloat16)
    for c in range(n_chunks):
        rows = slice(c * MIX_CHUNK, (c + 1) * MIX_CHUNK)
        for hd in range(HEADS):
            pair = (hd // 2) * LANES
            in_head = (lane >= (hd % 2) * QK_DIM) & (lane < (hd % 2 + 1) * QK_DIM)
            q_m = jnp.where(in_head, qk_ref[rows, pair:pair + LANES], 0.0).astype(jnp.bfloat16)
            k_m = jnp.where(in_head, qk_ref[rows, HEADS * QK_DIM + pair:HEADS * QK_DIM + pair + LANES]
                            * (QK_DIM ** -0.5), 0.0)
            k_t = k_m.T
            vcols = slice(hd * V_DIM, (hd + 1) * V_DIM)
            v_aug = jnp.concatenate([v_ref[rows, vcols], ones_aug], axis=1)
            neg_mm = jnp.broadcast_to(cols_t[rows, hd:hd + 1], (MIX_CHUNK, MIX_CHUNK))
            neg_m = jnp.broadcast_to(cols_t[rows, SUBLANES + hd:SUBLANES + hd + 1], (MIX_CHUNK, V_DIM))
            s_qk = jnp.dot(q_m, k_t.astype(jnp.bfloat16), preferred_element_type=jnp.float32)
            p = jnp.exp(jnp.where(causal, g_in[hd:hd + 1, rows] + neg_mm, -jnp.inf))
            w = (p * s_qk).astype(jnp.bfloat16)
            a_inter = jnp.exp(mm_prev[hd:hd + 1, rows] + neg_mm)
            st = state_ref[hd]
            inter = jnp.dot(q_m, st.astype(jnp.bfloat16), preferred_element_type=jnp.float32)
            nd = (jnp.dot(w, v_aug, preferred_element_type=jnp.float32)
                  + jnp.concatenate([a_inter, a_inter], axis=1) * inter)
            num = nd[:, :V_DIM]
            den = nd[:, V_DIM:]
            hh = num / jnp.maximum(jnp.abs(den), jnp.exp(neg_m))
            hh = hh * lax.rsqrt(jnp.mean(hh * hh, axis=-1, keepdims=True) + EPS)
            hh = hh * mh_ref[:, vcols] * og_ref[rows, vcols]
            ymix_ref[rows, sgu_w + hd * V_DIM:sgu_w + (hd + 1) * V_DIM] = hh.astype(jnp.bfloat16)
            k_w = (k_t * p_last[hd:hd + 1, rows]).astype(jnp.bfloat16)
            dec = decay[hd:hd + 1, rows]
            state_ref[hd] = (jnp.concatenate([dec, dec], axis=1) * st
                             + jnp.dot(k_w, v_aug, preferred_element_type=jnp.float32))

    o_ref[...] = x + jnp.dot(ymix_ref[...], wout_ref[...], preferred_element_type=jnp.float32)


def _mixer(x, gain, w_in, ln_g, ln_b, sgu_w, sgu_bias, conv_w, conv_b, gate_b, mh, w_out):
    bsz, seq, d = x.shape
    rows_t = MIX_ROWS
    assert seq % rows_t == 0 and rows_t % MIX_CHUNK == 0
    tok_spec = pl.BlockSpec((None, rows_t, d), lambda b, s: (b, s, 0))
    ins = (gain, w_in, ln_g, ln_b, sgu_w, sgu_bias, conv_w, conv_b, gate_b, mh, w_out)
    qk_w = 2 * HEADS * QK_DIM
    mv_w = HEADS * V_DIM
    return pl.pallas_call(
        _mixer_kernel,
        grid=(bsz, seq // rows_t),
        in_specs=[tok_spec] + [_resident(a.shape) for a in ins],
        out_specs=tok_spec,
        out_shape=jax.ShapeDtypeStruct(x.shape, jnp.float32),
        scratch_shapes=[
            pltpu.VMEM((HEADS, LANES, 2 * V_DIM), jnp.float32),
            pltpu.VMEM((SUBLANES, rows_t), jnp.float32),
            pltpu.VMEM((rows_t + SUBLANES, qk_w), jnp.float32),
            pltpu.VMEM((rows_t, qk_w), jnp.float32),
            pltpu.VMEM((rows_t, mv_w), jnp.bfloat16),
            pltpu.VMEM((rows_t, mv_w), jnp.float32),
            pltpu.VMEM((rows_t, 2 * mv_w), jnp.bfloat16),
        ],
        compiler_params=pltpu.CompilerParams(
            dimension_semantics=("arbitrary", "arbitrary"), vmem_limit_bytes=VMEM_LIMIT_BYTES),
        name="mixer",
    )(x, *ins)


def kernel(x, ffn1_norm, ffn1_w_gate, ffn1_w_up, ffn1_w_down, mix_norm, w_in, sgu_ln_g, sgu_ln_b, sgu_w, sgu_b, conv_w, conv_b, igate_b, fgate_b, mh_norm, w_out, ffn2_norm, ffn2_w_gate, ffn2_w_up, ffn2_w_down, final_norm):
    bsz, seq, d = x.shape
    bf = jnp.bfloat16
    row = lambda a: a.reshape(1, -1)
    fin = row(final_norm)
    for l in range(ffn1_norm.shape[0]):
        x = _ffn(x.reshape(bsz * seq, d), row(ffn1_norm[l]), ffn1_w_gate[l].astype(bf),
                 ffn1_w_up[l].astype(bf), ffn1_w_down[l].astype(bf), fin, False).reshape(bsz, seq, d)
        n_main = w_in.shape[2] - 2 * HEADS
        w_in_l = jnp.pad(w_in[l], ((0, 0), (0, LANES - 2 * HEADS))).astype(bf)
        assert w_in_l.shape[1] == n_main + LANES
        sgu_bias = jnp.repeat(sgu_b[l].T, LANES, axis=1)
        gate_b = jnp.broadcast_to(
            jnp.concatenate([igate_b[l], fgate_b[l]])[:, None], (2 * HEADS, MIX_ROWS))
        x = _mixer(x, row(mix_norm[l]), w_in_l, row(sgu_ln_g[l]), row(sgu_ln_b[l]), sgu_w[l], sgu_bias,
                   conv_w[l], row(conv_b[l]), gate_b, row(mh_norm[l]), w_out[l].astype(bf))
        last = l == ffn1_norm.shape[0] - 1
        x = _ffn(x.reshape(bsz * seq, d), row(ffn2_norm[l]), ffn2_w_gate[l].astype(bf),
                 ffn2_w_up[l].astype(bf), ffn2_w_down[l].astype(bf), fin, last).reshape(bsz, seq, d)
    return x
```

```python
import functools

import jax
import jax.numpy as jnp
from jax import lax
from jax.experimental import pallas as pl
from jax.experimental.pallas import tpu as pltpu

EPS = 1e-6
LANES = 128
SUBLANES = 8
VMEM_LIMIT_BYTES = 56 * 1024 * 1024

STREAM_CHUNK = 64
SGU_CHUNK = 128
SGU_GROUPS = 4
HEADS = 4
QK_DIM = 64
V_DIM = 128
MIX_CHUNK = 128

FFN_ROWS = 512
FFN_COLS = 256
MIX_ROWS = 512
OUT_CHUNKS = 2


def _resident(shape):
    zeros = (0,) * len(shape)
    return pl.BlockSpec(shape, lambda *_: zeros, pipeline_mode=pl.Buffered(1))


def _rms(x, gain):
    return x * lax.rsqrt(jnp.mean(x * x, axis=-1, keepdims=True) + EPS) * gain


def _ffn_kernel(x_ref, gain_ref, wg_ref, wu_ref, wd_ref, fin_ref, o_ref, act_ref, *, final_norm):
    x = x_ref[...]
    h = _rms(x, gain_ref[...]).astype(jnp.bfloat16)
    d_ff = wg_ref.shape[1]
    for c in range(d_ff // FFN_COLS):
        cols = slice(c * FFN_COLS, (c + 1) * FFN_COLS)
        g = jnp.dot(h, wg_ref[:, cols], preferred_element_type=jnp.float32)
        u = jnp.dot(h, wu_ref[:, cols], preferred_element_type=jnp.float32)
        act_ref[:, cols] = (g * jax.nn.sigmoid(g) * u).astype(jnp.bfloat16)
    y = jnp.dot(act_ref[...], wd_ref[...], preferred_element_type=jnp.float32)
    out = x + 0.5 * y
    if final_norm:
        out = _rms(out, fin_ref[...])
    o_ref[...] = out


def _ffn(x2d, gain, wg, wu, wd, fin, final_norm):
    n, d = x2d.shape
    d_ff = wg.shape[1]
    assert n % FFN_ROWS == 0 and d_ff % FFN_COLS == 0
    row_spec = pl.BlockSpec((FFN_ROWS, d), lambda i: (i, 0))
    return pl.pallas_call(
        functools.partial(_ffn_kernel, final_norm=final_norm),
        grid=(n // FFN_ROWS,),
        in_specs=[row_spec, _resident((1, d)), _resident((d, d_ff)), _resident((d, d_ff)),
                  _resident((d_ff, d)), _resident((1, d))],
        out_specs=row_spec,
        out_shape=jax.ShapeDtypeStruct((n, d), jnp.float32),
        scratch_shapes=[pltpu.VMEM((FFN_ROWS, d_ff), jnp.bfloat16)],
        compiler_params=pltpu.CompilerParams(
            dimension_semantics=("arbitrary",), vmem_limit_bytes=VMEM_LIMIT_BYTES),
        name="ffn_final" if final_norm else "ffn",
    )(x2d, gain, wg, wu, wd, fin)


def _scan_lanes(x, combine, fill):
    width = x.shape[1]
    lane = lax.broadcasted_iota(jnp.int32, x.shape, 1)
    d = 1
    while d < width:
        x = combine(x, jnp.where(lane >= d, pltpu.roll(x, d, 1), fill))
        d *= 2
    return x


def _mixer_kernel(x_ref, gain_ref, win_ref, lng_ref, lnb_ref, sw_ref, sb_ref, cw_ref, cb_ref,
                  gb_ref, mh_ref, wout_ref, o_ref,
                  state_ref, mcar_ref, ext_ref, qk_ref, v_ref, og_ref, ymix_ref):
    rows_t = x_ref.shape[0]
    n_chunks = rows_t // MIX_CHUNK
    sgu_w = SGU_GROUPS * LANES
    qk_w = 2 * HEADS * QK_DIM
    mv_w = HEADS * V_DIM
    o_uv, o_qk = 0, 2 * sgu_w
    o_v = o_qk + qk_w
    o_o = o_v + mv_w
    o_g = o_o + mv_w

    @pl.when(pl.program_id(1) == 0)
    def _():
        state_ref[...] = jnp.zeros_like(state_ref)
        mcar_ref[...] = jnp.zeros_like(mcar_ref)
        ext_ref[0:SUBLANES, :] = jnp.zeros((SUBLANES, qk_w), jnp.float32)

    x = x_ref[...]
    h = _rms(x, gain_ref[...]).astype(jnp.bfloat16)

    def proj(lo, width):
        return jnp.dot(h, win_ref[:, lo:lo + width], preferred_element_type=jnp.float32)

    z_gate = proj(o_g, LANES)
    ext_ref[SUBLANES:, :] = proj(o_qk, qk_w)
    uv = jax.nn.gelu(proj(o_uv, 2 * sgu_w))
    v_ref[...] = proj(o_v, mv_w).astype(jnp.bfloat16)
    og_ref[...] = jax.nn.sigmoid(proj(o_o, mv_w))

    u = uv[:, :sgu_w]
    v = uv[:, sgu_w:]
    vc = v - jnp.mean(v, axis=-1, keepdims=True)
    vn = vc * lax.rsqrt(jnp.mean(vc * vc, axis=-1, keepdims=True) + EPS) * lng_ref[...] + lnb_ref[...]
    vn = vn.astype(jnp.bfloat16)
    t_blk = lax.broadcasted_iota(jnp.int32, (SGU_CHUNK, SGU_CHUNK), 0) // STREAM_CHUNK
    s_blk = lax.broadcasted_iota(jnp.int32, (SGU_CHUNK, SGU_CHUNK), 1) // STREAM_CHUNK

    def spatial_gating():
        for g in range(SGU_GROUPS):
            cols = slice(g * LANES, (g + 1) * LANES)
            wm = jnp.where(t_blk >= s_blk, sw_ref[g], 0.0).astype(jnp.bfloat16)
            rhs = jnp.concatenate(
                [vn[c * SGU_CHUNK:(c + 1) * SGU_CHUNK, cols] for c in range(n_chunks)], axis=1)
            mixed = jnp.dot(wm, rhs, preferred_element_type=jnp.float32)
            for c in range(n_chunks):
                rows = slice(c * SGU_CHUNK, (c + 1) * SGU_CHUNK)
                gated = u[rows, cols] * (mixed[:, c * LANES:(c + 1) * LANES] + sb_ref[:, cols])
                ymix_ref[rows, cols] = gated.astype(jnp.bfloat16)

    conv = cb_ref[...]
    for j in range(cw_ref.shape[0]):
        conv = conv + cw_ref[j:j + 1, :] * ext_ref[pl.ds(SUBLANES - 3 + j, rows_t), :]
    qk_ref[...] = conv * jax.nn.sigmoid(conv)
    ext_ref[0:SUBLANES, :] = ext_ref[rows_t:rows_t + SUBLANES, :]

    gts = z_gate.T[0:SUBLANES, :] + gb_ref[...]
    logf = pltpu.roll(jax.nn.log_sigmoid(gts), HEADS, 0)
    b_cum = _scan_lanes(logf, jnp.add, 0.0)
    g_in = gts - b_cum
    mm = jnp.maximum(mcar_ref[...], _scan_lanes(g_in, jnp.maximum, -jnp.inf))
    mm_last = jnp.concatenate(
        [jnp.broadcast_to(mm[:, (c + 1) * MIX_CHUNK - 1:(c + 1) * MIX_CHUNK], (SUBLANES, MIX_CHUNK))
         for c in range(n_chunks)], axis=1)
    mm_prev = jnp.concatenate([mcar_ref[:, :MIX_CHUNK], mm_last[:, :rows_t - MIX_CHUNK]], axis=1)
    decay = jnp.exp(mm_prev - mm_last)
    p_last = jnp.exp(g_in - mm_last)
    m_end = b_cum + mm
    mcar_ref[...] = jnp.broadcast_to(m_end[:, rows_t - 1:rows_t], (SUBLANES, rows_t))
    cols_t = jnp.concatenate(
        [-mm, -m_end, jnp.zeros((LANES - 2 * SUBLANES, rows_t), jnp.float32)], axis=0).T

    lane = lax.broadcasted_iota(jnp.int32, (1, LANES), 1)
    t_idx = lax.broadcasted_iota(jnp.int32, (MIX_CHUNK, MIX_CHUNK), 0)
    s_idx = lax.broadcasted_iota(jnp.int32, (MIX_CHUNK, MIX_CHUNK), 1)
    causal = s_idx <= t_idx
    ones_aug = jnp.ones((MIX_CHUNK, V_DIM), jnp.bfloat16)
    for c in range(n_chunks):
        rows = slice(c * MIX_CHUNK, (c + 1) * MIX_CHUNK)
        for hd in range(HEADS):
            pair = (hd // 2) * LANES
            in_head = (lane >= (hd % 2) * QK_DIM) & (lane < (hd % 2 + 1) * QK_DIM)
            q_m = jnp.where(in_head, qk_ref[rows, pair:pair + LANES], 0.0).astype(jnp.bfloat16)
            k_m = jnp.where(in_head, qk_ref[rows, HEADS * QK_DIM + pair:HEADS * QK_DIM + pair + LANES]
                            * (QK_DIM ** -0.5), 0.0)
            k_t = k_m.T
            vcols = slice(hd * V_DIM, (hd + 1) * V_DIM)
            v_aug = jnp.concatenate([v_ref[rows, vcols], ones_aug], axis=1)
            neg_mm = jnp.broadcast_to(cols_t[rows, hd:hd + 1], (MIX_CHUNK, MIX_CHUNK))
            neg_m = jnp.broadcast_to(cols_t[rows, SUBLANES + hd:SUBLANES + hd + 1], (MIX_CHUNK, V_DIM))
            s_qk = jnp.dot(q_m, k_t.astype(jnp.bfloat16), preferred_element_type=jnp.float32)
            p = jnp.exp(jnp.where(causal, g_in[hd:hd + 1, rows] + neg_mm, -jnp.inf))
            w = (p * s_qk).astype(jnp.bfloat16)
            a_inter = jnp.exp(mm_prev[hd:hd + 1, rows] + neg_mm)
            st = state_ref[hd]
            inter = jnp.dot(q_m, st.astype(jnp.bfloat16), preferred_element_type=jnp.float32)
            nd = (jnp.dot(w, v_aug, preferred_element_type=jnp.float32)
                  + jnp.concatenate([a_inter, a_inter], axis=1) * inter)
            num = nd[:, :V_DIM]
            den = nd[:, V_DIM:]
            hh = num / jnp.maximum(jnp.abs(den), jnp.exp(neg_m))
            hh = hh * lax.rsqrt(jnp.mean(hh * hh, axis=-1, keepdims=True) + EPS)
            hh = hh * mh_ref[:, vcols] * og_ref[rows, vcols]
            ymix_ref[rows, sgu_w + hd * V_DIM:sgu_w + (hd + 1) * V_DIM] = hh.astype(jnp.bfloat16)
            k_w = (k_t * p_last[hd:hd + 1, rows]).astype(jnp.bfloat16)
            dec = decay[hd:hd + 1, rows]
            state_ref[hd] = (jnp.concatenate([dec, dec], axis=1) * st
                             + jnp.dot(k_w, v_aug, preferred_element_type=jnp.float32))
        if c + 1 == OUT_CHUNKS:
            spatial_gating()
        if (c + 1) % OUT_CHUNKS == 0:
            done = slice((c + 1 - OUT_CHUNKS) * MIX_CHUNK, (c + 1) * MIX_CHUNK)
            o_ref[done, :] = x_ref[done, :] + jnp.dot(
                ymix_ref[done, :], wout_ref[...], preferred_element_type=jnp.float32)


def _mixer(x, gain, w_in, ln_g, ln_b, sgu_w, sgu_bias, conv_w, conv_b, gate_b, mh, w_out):
    bsz, seq, d = x.shape
    rows_t = MIX_ROWS
    assert seq % rows_t == 0 and rows_t % MIX_CHUNK == 0
    tok_spec = pl.BlockSpec((None, rows_t, d), lambda b, s: (b, s, 0))
    ins = (gain, w_in, ln_g, ln_b, sgu_w, sgu_bias, conv_w, conv_b, gate_b, mh, w_out)
    qk_w = 2 * HEADS * QK_DIM
    mv_w = HEADS * V_DIM
    return pl.pallas_call(
        _mixer_kernel,
        grid=(bsz, seq // rows_t),
        in_specs=[tok_spec] + [_resident(a.shape) for a in ins],
        out_specs=tok_spec,
        out_shape=jax.ShapeDtypeStruct(x.shape, jnp.float32),
        scratch_shapes=[
            pltpu.VMEM((HEADS, LANES, 2 * V_DIM), jnp.float32),
            pltpu.VMEM((SUBLANES, rows_t), jnp.float32),
            pltpu.VMEM((rows_t + SUBLANES, qk_w), jnp.float32),
            pltpu.VMEM((rows_t, qk_w), jnp.float32),
            pltpu.VMEM((rows_t, mv_w), jnp.bfloat16),
            pltpu.VMEM((rows_t, mv_w), jnp.float32),
            pltpu.VMEM((rows_t, 2 * mv_w), jnp.bfloat16),
        ],
        compiler_params=pltpu.CompilerParams(
            dimension_semantics=("arbitrary", "arbitrary"), vmem_limit_bytes=VMEM_LIMIT_BYTES),
        name="mixer",
    )(x, *ins)


def kernel(x, ffn1_norm, ffn1_w_gate, ffn1_w_up, ffn1_w_down, mix_norm, w_in, sgu_ln_g, sgu_ln_b, sgu_w, sgu_b, conv_w, conv_b, igate_b, fgate_b, mh_norm, w_out, ffn2_norm, ffn2_w_gate, ffn2_w_up, ffn2_w_down, final_norm):
    bsz, seq, d = x.shape
    bf = jnp.bfloat16
    row = lambda a: a.reshape(1, -1)
    fin = row(final_norm)
    for l in range(ffn1_norm.shape[0]):
        x = _ffn(x.reshape(bsz * seq, d), row(ffn1_norm[l]), ffn1_w_gate[l].astype(bf),
                 ffn1_w_up[l].astype(bf), ffn1_w_down[l].astype(bf), fin, False).reshape(bsz, seq, d)
        n_main = w_in.shape[2] - 2 * HEADS
        w_in_l = jnp.pad(w_in[l], ((0, 0), (0, LANES - 2 * HEADS))).astype(bf)
        assert w_in_l.shape[1] == n_main + LANES
        sgu_bias = jnp.repeat(sgu_b[l].T, LANES, axis=1)
        gate_b = jnp.broadcast_to(
            jnp.concatenate([igate_b[l], fgate_b[l]])[:, None], (2 * HEADS, MIX_ROWS))
        x = _mixer(x, row(mix_norm[l]), w_in_l, row(sgu_ln_g[l]), row(sgu_ln_b[l]), sgu_w[l], sgu_bias,
                   conv_w[l], row(conv_b[l]), gate_b, row(mh_norm[l]), w_out[l].astype(bf))
        last = l == ffn1_norm.shape[0] - 1
        x = _ffn(x.reshape(bsz * seq, d), row(ffn2_norm[l]), ffn2_w_gate[l].astype(bf),
                 ffn2_w_up[l].astype(bf), ffn2_w_down[l].astype(bf), fin, last).reshape(bsz, seq, d)
    return x
```

```python
import functools

import jax
import jax.numpy as jnp
from jax import lax
from jax.experimental import pallas as pl
from jax.experimental.pallas import tpu as pltpu

EPS = 1e-6
LANES = 128
SUBLANES = 8
VMEM_LIMIT_BYTES = 56 * 1024 * 1024

STREAM_CHUNK = 64
SGU_CHUNK = 128
SGU_GROUPS = 4
HEADS = 4
QK_DIM = 64
V_DIM = 128
MIX_CHUNK = 128

FFN_ROWS = 1024
FFN_COLS = 256
MIX_ROWS = 512
OUT_CHUNKS = 2
LOG2E = 1.4426950408889634


def _resident(shape):
    zeros = (0,) * len(shape)
    return pl.BlockSpec(shape, lambda *_: zeros, pipeline_mode=pl.Buffered(1))


def _rms(x, gain):
    return x * lax.rsqrt(jnp.mean(x * x, axis=-1, keepdims=True) + EPS) * gain


def _ffn_kernel(x_ref, gain_ref, wg_ref, wu_ref, wd_ref, fin_ref, o_ref, act_ref, *, final_norm):
    x = x_ref[...]
    h = _rms(x, gain_ref[...]).astype(jnp.bfloat16)
    d_ff = wg_ref.shape[1]
    for c in range(d_ff // FFN_COLS):
        cols = slice(c * FFN_COLS, (c + 1) * FFN_COLS)
        g = jnp.dot(h, wg_ref[:, cols], preferred_element_type=jnp.float32)
        u = jnp.dot(h, wu_ref[:, cols], preferred_element_type=jnp.float32)
        act_ref[:, cols] = (g * jax.nn.sigmoid(g) * u).astype(jnp.bfloat16)
    y = jnp.dot(act_ref[...], wd_ref[...], preferred_element_type=jnp.float32)
    out = x + 0.5 * y
    if final_norm:
        out = _rms(out, fin_ref[...])
    o_ref[...] = out


def _ffn(x2d, gain, wg, wu, wd, fin, final_norm):
    n, d = x2d.shape
    d_ff = wg.shape[1]
    assert n % FFN_ROWS == 0 and d_ff % FFN_COLS == 0
    row_spec = pl.BlockSpec((FFN_ROWS, d), lambda i: (i, 0))
    return pl.pallas_call(
        functools.partial(_ffn_kernel, final_norm=final_norm),
        grid=(n // FFN_ROWS,),
        in_specs=[row_spec, _resident((1, d)), _resident((d, d_ff)), _resident((d, d_ff)),
                  _resident((d_ff, d)), _resident((1, d))],
        out_specs=row_spec,
        out_shape=jax.ShapeDtypeStruct((n, d), jnp.float32),
        scratch_shapes=[pltpu.VMEM((FFN_ROWS, d_ff), jnp.bfloat16)],
        compiler_params=pltpu.CompilerParams(
            dimension_semantics=("arbitrary",), vmem_limit_bytes=VMEM_LIMIT_BYTES),
        name="ffn_final" if final_norm else "ffn",
    )(x2d, gain, wg, wu, wd, fin)


def _scan_lanes(x, combine, fill):
    width = x.shape[1]
    lane = lax.broadcasted_iota(jnp.int32, x.shape, 1)
    d = 1
    while d < width:
        x = combine(x, jnp.where(lane >= d, pltpu.roll(x, d, 1), fill))
        d *= 2
    return x


def _mixer_kernel(x_ref, gain_ref, win_ref, lng_ref, lnb_ref, sw_ref, sb_ref, cw_ref, cb_ref,
                  gb_ref, mh_ref, wout_ref, o_ref,
                  state_ref, mcar_ref, ext_ref, qk_ref, v_ref, og_ref, ymix_ref):
    rows_t = x_ref.shape[0]
    n_chunks = rows_t // MIX_CHUNK
    sgu_w = SGU_GROUPS * LANES
    qk_w = 2 * HEADS * QK_DIM
    mv_w = HEADS * V_DIM
    o_uv, o_qk = 0, 2 * sgu_w
    o_v = o_qk + qk_w
    o_o = o_v + mv_w
    o_g = o_o + mv_w

    @pl.when(pl.program_id(1) == 0)
    def _():
        state_ref[...] = jnp.zeros_like(state_ref)
        mcar_ref[...] = jnp.zeros_like(mcar_ref)
        ext_ref[:, 0:SUBLANES, :] = jnp.zeros((qk_w // LANES, SUBLANES, LANES), jnp.float32)

    x = x_ref[...]
    h = _rms(x, gain_ref[...]).astype(jnp.bfloat16)

    def proj(lo, width):
        return jnp.dot(h, win_ref[:, lo:lo + width], preferred_element_type=jnp.float32)

    z_gate = proj(o_g, LANES)
    z_qk = proj(o_qk, qk_w)
    for t in range(qk_w // LANES):
        ext_ref[t, SUBLANES:, :] = z_qk[:, t * LANES:(t + 1) * LANES]
    uv = jax.nn.gelu(proj(o_uv, 2 * sgu_w))
    v_ref[...] = proj(o_v, mv_w).astype(jnp.bfloat16)
    og_ref[...] = jax.nn.sigmoid(proj(o_o, mv_w))

    u = uv[:, :sgu_w]
    v = uv[:, sgu_w:]
    vc = v - jnp.mean(v, axis=-1, keepdims=True)
    vn = vc * lax.rsqrt(jnp.mean(vc * vc, axis=-1, keepdims=True) + EPS) * lng_ref[...] + lnb_ref[...]
    vn = vn.astype(jnp.bfloat16)
    t_blk = lax.broadcasted_iota(jnp.int32, (SGU_CHUNK, SGU_CHUNK), 0) // STREAM_CHUNK
    s_blk = lax.broadcasted_iota(jnp.int32, (SGU_CHUNK, SGU_CHUNK), 1) // STREAM_CHUNK

    def spatial_gating():
        for g in range(SGU_GROUPS):
            cols = slice(g * LANES, (g + 1) * LANES)
            wm = jnp.where(t_blk >= s_blk, sw_ref[g], 0.0).astype(jnp.bfloat16)
            rhs = jnp.concatenate(
                [vn[c * SGU_CHUNK:(c + 1) * SGU_CHUNK, cols] for c in range(n_chunks)], axis=1)
            mixed = jnp.dot(wm, rhs, preferred_element_type=jnp.float32)
            for c in range(n_chunks):
                rows = slice(c * SGU_CHUNK, (c + 1) * SGU_CHUNK)
                gated = u[rows, cols] * (mixed[:, c * LANES:(c + 1) * LANES] + sb_ref[:, cols])
                ymix_ref[rows, cols] = gated.astype(jnp.bfloat16)

    for t in range(qk_w // LANES):
        cols = slice(t * LANES, (t + 1) * LANES)
        conv = cb_ref[:, cols]
        for j in range(cw_ref.shape[0]):
            conv = conv + cw_ref[j:j + 1, cols] * ext_ref[t, pl.ds(SUBLANES - 3 + j, rows_t), :]
        qk_ref[:, cols] = conv * jax.nn.sigmoid(conv)
        ext_ref[t, 0:SUBLANES, :] = ext_ref[t, rows_t:rows_t + SUBLANES, :]

    gts = z_gate.T[0:SUBLANES, :] + gb_ref[...]
    logf = pltpu.roll(jax.nn.log_sigmoid(gts), HEADS, 0)
    b_cum = _scan_lanes(logf, jnp.add, 0.0)
    g_in = gts - b_cum
    mm = jnp.maximum(mcar_ref[...], _scan_lanes(g_in, jnp.maximum, -jnp.inf))
    mm_last = jnp.concatenate(
        [jnp.broadcast_to(mm[:, (c + 1) * MIX_CHUNK - 1:(c + 1) * MIX_CHUNK], (SUBLANES, MIX_CHUNK))
         for c in range(n_chunks)], axis=1)
    mm_prev = jnp.concatenate([mcar_ref[:, :MIX_CHUNK], mm_last[:, :rows_t - MIX_CHUNK]], axis=1)
    decay = jnp.exp(mm_prev - mm_last)
    p_last = jnp.exp(g_in - mm_last)
    m_end = b_cum + mm
    mcar_ref[...] = jnp.broadcast_to(m_end[:, rows_t - 1:rows_t], (SUBLANES, rows_t))
    g_in2 = g_in * LOG2E
    mm_prev2 = mm_prev * LOG2E
    cols_t = jnp.concatenate(
        [mm * -LOG2E, m_end * -LOG2E, jnp.zeros((LANES - 2 * SUBLANES, rows_t), jnp.float32)], axis=0).T

    lane = lax.broadcasted_iota(jnp.int32, (1, LANES), 1)
    t_idx = lax.broadcasted_iota(jnp.int32, (MIX_CHUNK, MIX_CHUNK), 0)
    s_idx = lax.broadcasted_iota(jnp.int32, (MIX_CHUNK, MIX_CHUNK), 1)
    causal = s_idx <= t_idx
    ones_aug = jnp.ones((MIX_CHUNK, V_DIM), jnp.bfloat16)
    scores, local, vaugs, qs = {}, {}, {}, {}
    for c in range(n_chunks):
        rows = slice(c * MIX_CHUNK, (c + 1) * MIX_CHUNK)
        for hd in range(HEADS):
            pair = (hd // 2) * LANES
            in_head = (lane >= (hd % 2) * QK_DIM) & (lane < (hd % 2 + 1) * QK_DIM)
            k_scale = jnp.where(in_head, QK_DIM ** -0.5, 0.0)
            q = qk_ref[rows, pair:pair + LANES]
            k_t = (qk_ref[rows, HEADS * QK_DIM + pair:HEADS * QK_DIM + pair + LANES] * k_scale).T
            vcols = slice(hd * V_DIM, (hd + 1) * V_DIM)
            v_aug = jnp.concatenate([v_ref[rows, vcols], ones_aug], axis=1)
            scores[c, hd] = jnp.dot(q.astype(jnp.bfloat16), k_t.astype(jnp.bfloat16),
                                    preferred_element_type=jnp.float32)
            k_w = (k_t * p_last[hd:hd + 1, rows]).astype(jnp.bfloat16)
            local[c, hd] = jnp.dot(k_w, v_aug, preferred_element_type=jnp.float32)
            vaugs[c, hd] = v_aug
            qs[c, hd] = q
    for c in range(n_chunks):
        rows = slice(c * MIX_CHUNK, (c + 1) * MIX_CHUNK)
        for hd in range(HEADS):
            vcols = slice(hd * V_DIM, (hd + 1) * V_DIM)
            neg_mm = jnp.broadcast_to(cols_t[rows, hd:hd + 1], (MIX_CHUNK, MIX_CHUNK))
            neg_m = jnp.broadcast_to(cols_t[rows, SUBLANES + hd:SUBLANES + hd + 1], (MIX_CHUNK, V_DIM))
            p = jnp.exp2(jnp.where(causal, g_in2[hd:hd + 1, rows] + neg_mm, -jnp.inf))
            w = (p * scores[c, hd]).astype(jnp.bfloat16)
            a_inter = jnp.exp2(mm_prev2[hd:hd + 1, rows] + neg_mm)
            st = state_ref[hd]
            lhs = jnp.concatenate([w, (qs[c, hd] * a_inter).astype(jnp.bfloat16)], axis=1)
            rhs = jnp.concatenate([vaugs[c, hd], st.astype(jnp.bfloat16)], axis=0)
            nd = jnp.dot(lhs, rhs, preferred_element_type=jnp.float32)
            num = nd[:, :V_DIM]
            den = nd[:, V_DIM:]
            hh = num / jnp.maximum(jnp.abs(den), jnp.exp2(neg_m))
            hh = hh * lax.rsqrt(jnp.mean(hh * hh, axis=-1, keepdims=True) + EPS)
            hh = hh * mh_ref[:, vcols] * og_ref[rows, vcols]
            ymix_ref[rows, sgu_w + hd * V_DIM:sgu_w + (hd + 1) * V_DIM] = hh.astype(jnp.bfloat16)
            dec = decay[hd:hd + 1, rows]
            state_ref[hd] = jnp.concatenate([dec, dec], axis=1) * st + local[c, hd]
        if c + 1 == OUT_CHUNKS:
            spatial_gating()
        if (c + 1) % OUT_CHUNKS == 0:
            done = slice((c + 1 - OUT_CHUNKS) * MIX_CHUNK, (c + 1) * MIX_CHUNK)
            o_ref[done, :] = x_ref[done, :] + jnp.dot(
                ymix_ref[done, :], wout_ref[...], preferred_element_type=jnp.float32)


def _mixer(x, gain, w_in, ln_g, ln_b, sgu_w, sgu_bias, conv_w, conv_b, gate_b, mh, w_out):
    bsz, seq, d = x.shape
    rows_t = MIX_ROWS
    assert seq % rows_t == 0 and rows_t % MIX_CHUNK == 0
    tok_spec = pl.BlockSpec((None, rows_t, d), lambda b, s: (b, s, 0))
    ins = (gain, w_in, ln_g, ln_b, sgu_w, sgu_bias, conv_w, conv_b, gate_b, mh, w_out)
    qk_w = 2 * HEADS * QK_DIM
    mv_w = HEADS * V_DIM
    return pl.pallas_call(
        _mixer_kernel,
        grid=(bsz, seq // rows_t),
        in_specs=[tok_spec] + [_resident(a.shape) for a in ins],
        out_specs=tok_spec,
        out_shape=jax.ShapeDtypeStruct(x.shape, jnp.float32),
        scratch_shapes=[
            pltpu.VMEM((HEADS, LANES, 2 * V_DIM), jnp.float32),
            pltpu.VMEM((SUBLANES, rows_t), jnp.float32),
            pltpu.VMEM((qk_w // LANES, rows_t + SUBLANES, LANES), jnp.float32),
            pltpu.VMEM((rows_t, qk_w), jnp.float32),
            pltpu.VMEM((rows_t, mv_w), jnp.bfloat16),
            pltpu.VMEM((rows_t, mv_w), jnp.float32),
            pltpu.VMEM((rows_t, 2 * mv_w), jnp.bfloat16),
        ],
        compiler_params=pltpu.CompilerParams(
            dimension_semantics=("arbitrary", "arbitrary"), vmem_limit_bytes=VMEM_LIMIT_BYTES),
        name="mixer",
    )(x, *ins)


def kernel(x, ffn1_norm, ffn1_w_gate, ffn1_w_up, ffn1_w_down, mix_norm, w_in, sgu_ln_g, sgu_ln_b, sgu_w, sgu_b, conv_w, conv_b, igate_b, fgate_b, mh_norm, w_out, ffn2_norm, ffn2_w_gate, ffn2_w_up, ffn2_w_down, final_norm):
    bsz, seq, d = x.shape
    bf = jnp.bfloat16
    row = lambda a: a.reshape(1, -1)
    fin = row(final_norm)
    for l in range(ffn1_norm.shape[0]):
        x = _ffn(x.reshape(bsz * seq, d), row(ffn1_norm[l]), ffn1_w_gate[l].astype(bf),
                 ffn1_w_up[l].astype(bf), ffn1_w_down[l].astype(bf), fin, False).reshape(bsz, seq, d)
        n_main = w_in.shape[2] - 2 * HEADS
        w_in_l = jnp.pad(w_in[l], ((0, 0), (0, LANES - 2 * HEADS))).astype(bf)
        assert w_in_l.shape[1] == n_main + LANES
        sgu_bias = jnp.repeat(sgu_b[l].T, LANES, axis=1)
        gate_b = jnp.broadcast_to(
            jnp.concatenate([igate_b[l], fgate_b[l]])[:, None], (2 * HEADS, MIX_ROWS))
        x = _mixer(x, row(mix_norm[l]), w_in_l, row(sgu_ln_g[l]), row(sgu_ln_b[l]), sgu_w[l], sgu_bias,
                   conv_w[l], row(conv_b[l]), gate_b, row(mh_norm[l]), w_out[l].astype(bf))
        last = l == ffn1_norm.shape[0] - 1
        x = _ffn(x.reshape(bsz * seq, d), row(ffn2_norm[l]), ffn2_w_gate[l].astype(bf),
                 ffn2_w_up[l].astype(bf), ffn2_w_down[l].astype(bf), fin, last).reshape(bsz, seq, d)
    return x
```

```python
import functools

import jax
import jax.numpy as jnp
from jax import lax
from jax.experimental import pallas as pl
from jax.experimental.pallas import tpu as pltpu

EPS = 1e-6
LANES = 128
SUBLANES = 8
VMEM_LIMIT_BYTES = 56 * 1024 * 1024

STREAM_CHUNK = 64
SGU_CHUNK = 128
SGU_GROUPS = 4
HEADS = 4
QK_DIM = 64
V_DIM = 128
MIX_CHUNK = 128

FFN_ROWS = 1024
FFN_COLS = 256
MIX_ROWS = 1024
OUT_CHUNKS = 4
LOG2E = 1.4426950408889634


def _resident(shape):
    zeros = (0,) * len(shape)
    return pl.BlockSpec(shape, lambda *_: zeros, pipeline_mode=pl.Buffered(1))


def _rms(x, gain):
    return x * lax.rsqrt(jnp.mean(x * x, axis=-1, keepdims=True) + EPS) * gain


def _ffn_kernel(x_ref, gain_ref, wg_ref, wu_ref, wd_ref, fin_ref, o_ref, act_ref, *, final_norm):
    x = x_ref[...]
    h = _rms(x, gain_ref[...]).astype(jnp.bfloat16)
    d_ff = wg_ref.shape[1]
    for c in range(d_ff // FFN_COLS):
        cols = slice(c * FFN_COLS, (c + 1) * FFN_COLS)
        g = jnp.dot(h, wg_ref[:, cols], preferred_element_type=jnp.float32)
        u = jnp.dot(h, wu_ref[:, cols], preferred_element_type=jnp.float32)
        act_ref[:, cols] = (g * jax.nn.sigmoid(g) * u).astype(jnp.bfloat16)
    y = jnp.dot(act_ref[...], wd_ref[...], preferred_element_type=jnp.float32)
    out = x + 0.5 * y
    if final_norm:
        out = _rms(out, fin_ref[...])
    o_ref[...] = out


def _ffn(x2d, gain, wg, wu, wd, fin, final_norm):
    n, d = x2d.shape
    d_ff = wg.shape[1]
    assert n % FFN_ROWS == 0 and d_ff % FFN_COLS == 0
    row_spec = pl.BlockSpec((FFN_ROWS, d), lambda i: (i, 0))
    return pl.pallas_call(
        functools.partial(_ffn_kernel, final_norm=final_norm),
        grid=(n // FFN_ROWS,),
        in_specs=[row_spec, _resident((1, d)), _resident((d, d_ff)), _resident((d, d_ff)),
                  _resident((d_ff, d)), _resident((1, d))],
        out_specs=row_spec,
        out_shape=jax.ShapeDtypeStruct((n, d), jnp.float32),
        scratch_shapes=[pltpu.VMEM((FFN_ROWS, d_ff), jnp.bfloat16)],
        compiler_params=pltpu.CompilerParams(
            dimension_semantics=("arbitrary",), vmem_limit_bytes=VMEM_LIMIT_BYTES),
        name="ffn_final" if final_norm else "ffn",
    )(x2d, gain, wg, wu, wd, fin)


def _scan_lanes(x, combine, fill):
    width = x.shape[1]
    lane = lax.broadcasted_iota(jnp.int32, x.shape, 1)
    d = 1
    while d < width:
        x = combine(x, jnp.where(lane >= d, pltpu.roll(x, d, 1), fill))
        d *= 2
    return x


def _mixer_kernel(x_ref, gain_ref, win_ref, lng_ref, lnb_ref, sw_ref, sb_ref, cw_ref, cb_ref,
                  gb_ref, mh_ref, wout_ref, o_ref,
                  state_ref, mcar_ref, ext_ref, qk_ref, v_ref, og_ref, ymix_ref):
    rows_t = x_ref.shape[0]
    n_chunks = rows_t // MIX_CHUNK
    sgu_w = SGU_GROUPS * LANES
    qk_w = 2 * HEADS * QK_DIM
    mv_w = HEADS * V_DIM
    o_uv, o_qk = 0, 2 * sgu_w
    o_v = o_qk + qk_w
    o_o = o_v + mv_w
    o_g = o_o + mv_w

    @pl.when(pl.program_id(1) == 0)
    def _():
        state_ref[...] = jnp.zeros_like(state_ref)
        mcar_ref[...] = jnp.zeros_like(mcar_ref)
        ext_ref[:, 0:SUBLANES, :] = jnp.zeros((qk_w // LANES, SUBLANES, LANES), jnp.float32)

    x = x_ref[...]
    h = _rms(x, gain_ref[...]).astype(jnp.bfloat16)

    def proj(lo, width):
        return jnp.dot(h, win_ref[:, lo:lo + width], preferred_element_type=jnp.float32)

    z_gate = proj(o_g, LANES)
    z_qk = proj(o_qk, qk_w)
    for t in range(qk_w // LANES):
        ext_ref[t, SUBLANES:, :] = z_qk[:, t * LANES:(t + 1) * LANES]
    uv = jax.nn.gelu(proj(o_uv, 2 * sgu_w))
    v_ref[...] = proj(o_v, mv_w).astype(jnp.bfloat16)
    og_ref[...] = jax.nn.sigmoid(proj(o_o, mv_w))

    u = uv[:, :sgu_w]
    v = uv[:, sgu_w:]
    vc = v - jnp.mean(v, axis=-1, keepdims=True)
    vn = vc * lax.rsqrt(jnp.mean(vc * vc, axis=-1, keepdims=True) + EPS) * lng_ref[...] + lnb_ref[...]
    vn = vn.astype(jnp.bfloat16)
    t_blk = lax.broadcasted_iota(jnp.int32, (SGU_CHUNK, SGU_CHUNK), 0) // STREAM_CHUNK
    s_blk = lax.broadcasted_iota(jnp.int32, (SGU_CHUNK, SGU_CHUNK), 1) // STREAM_CHUNK

    def spatial_gating():
        for g in range(SGU_GROUPS):
            cols = slice(g * LANES, (g + 1) * LANES)
            wm = jnp.where(t_blk >= s_blk, sw_ref[g], 0.0).astype(jnp.bfloat16)
            rhs = jnp.concatenate(
                [vn[c * SGU_CHUNK:(c + 1) * SGU_CHUNK, cols] for c in range(n_chunks)], axis=1)
            mixed = jnp.dot(wm, rhs, preferred_element_type=jnp.float32)
            for c in range(n_chunks):
                rows = slice(c * SGU_CHUNK, (c + 1) * SGU_CHUNK)
                gated = u[rows, cols] * (mixed[:, c * LANES:(c + 1) * LANES] + sb_ref[:, cols])
                ymix_ref[rows, cols] = gated.astype(jnp.bfloat16)

    for t in range(qk_w // LANES):
        cols = slice(t * LANES, (t + 1) * LANES)
        conv = cb_ref[:, cols]
        for j in range(cw_ref.shape[0]):
            conv = conv + cw_ref[j:j + 1, cols] * ext_ref[t, pl.ds(SUBLANES - 3 + j, rows_t), :]
        qk_ref[:, cols] = conv * jax.nn.sigmoid(conv)
        ext_ref[t, 0:SUBLANES, :] = ext_ref[t, rows_t:rows_t + SUBLANES, :]

    gts = z_gate.T[0:SUBLANES, :] + gb_ref[...]
    logf = pltpu.roll(jax.nn.log_sigmoid(gts), HEADS, 0)
    b_cum = _scan_lanes(logf, jnp.add, 0.0)
    g_in = gts - b_cum
    mm = jnp.maximum(mcar_ref[...], _scan_lanes(g_in, jnp.maximum, -jnp.inf))
    mm_last = jnp.concatenate(
        [jnp.broadcast_to(mm[:, (c + 1) * MIX_CHUNK - 1:(c + 1) * MIX_CHUNK], (SUBLANES, MIX_CHUNK))
         for c in range(n_chunks)], axis=1)
    mm_prev = jnp.concatenate([mcar_ref[:, :MIX_CHUNK], mm_last[:, :rows_t - MIX_CHUNK]], axis=1)
    decay = jnp.exp(mm_prev - mm_last)
    p_last = jnp.exp(g_in - mm_last)
    m_end = b_cum + mm
    mcar_ref[...] = jnp.broadcast_to(m_end[:, rows_t - 1:rows_t], (SUBLANES, rows_t))
    g_in2 = g_in * LOG2E
    mm_prev2 = mm_prev * LOG2E
    cols_t = jnp.concatenate(
        [mm * -LOG2E, m_end * -LOG2E, jnp.zeros((LANES - 2 * SUBLANES, rows_t), jnp.float32)], axis=0).T

    lane = lax.broadcasted_iota(jnp.int32, (1, LANES), 1)
    t_idx = lax.broadcasted_iota(jnp.int32, (MIX_CHUNK, MIX_CHUNK), 0)
    s_idx = lax.broadcasted_iota(jnp.int32, (MIX_CHUNK, MIX_CHUNK), 1)
    causal = s_idx <= t_idx
    ones_aug = jnp.ones((MIX_CHUNK, V_DIM), jnp.bfloat16)
    scores, local, vaugs, qs = {}, {}, {}, {}
    for c in range(n_chunks):
        rows = slice(c * MIX_CHUNK, (c + 1) * MIX_CHUNK)
        for hd in range(HEADS):
            pair = (hd // 2) * LANES
            in_head = (lane >= (hd % 2) * QK_DIM) & (lane < (hd % 2 + 1) * QK_DIM)
            k_scale = jnp.where(in_head, QK_DIM ** -0.5, 0.0)
            q = qk_ref[rows, pair:pair + LANES]
            k_t = (qk_ref[rows, HEADS * QK_DIM + pair:HEADS * QK_DIM + pair + LANES] * k_scale).T
            vcols = slice(hd * V_DIM, (hd + 1) * V_DIM)
            v_aug = jnp.concatenate([v_ref[rows, vcols], ones_aug], axis=1)
            scores[c, hd] = jnp.dot(q.astype(jnp.bfloat16), k_t.astype(jnp.bfloat16),
                                    preferred_element_type=jnp.float32)
            k_w = (k_t * p_last[hd:hd + 1, rows]).astype(jnp.bfloat16)
            local[c, hd] = jnp.dot(k_w, v_aug, preferred_element_type=jnp.float32)
            vaugs[c, hd] = v_aug
            qs[c, hd] = q
    spatial_gating()
    for c in range(n_chunks):
        rows = slice(c * MIX_CHUNK, (c + 1) * MIX_CHUNK)
        for hd in range(HEADS):
            vcols = slice(hd * V_DIM, (hd + 1) * V_DIM)
            neg_mm = jnp.broadcast_to(cols_t[rows, hd:hd + 1], (MIX_CHUNK, MIX_CHUNK))
            neg_m = jnp.broadcast_to(cols_t[rows, SUBLANES + hd:SUBLANES + hd + 1], (MIX_CHUNK, V_DIM))
            p = jnp.exp2(jnp.where(causal, g_in2[hd:hd + 1, rows] + neg_mm, -jnp.inf))
            w = (p * scores[c, hd]).astype(jnp.bfloat16)
            a_inter = jnp.exp2(mm_prev2[hd:hd + 1, rows] + neg_mm)
            st = state_ref[hd]
            lhs = jnp.concatenate([w, (qs[c, hd] * a_inter).astype(jnp.bfloat16)], axis=1)
            rhs = jnp.concatenate([vaugs[c, hd], st.astype(jnp.bfloat16)], axis=0)
            nd = jnp.dot(lhs, rhs, preferred_element_type=jnp.float32)
            num = nd[:, :V_DIM]
            den = nd[:, V_DIM:]
            hh = num / jnp.maximum(jnp.abs(den), jnp.exp2(neg_m))
            hh = hh * lax.rsqrt(jnp.mean(hh * hh, axis=-1, keepdims=True) + EPS)
            hh = hh * mh_ref[:, vcols] * og_ref[rows, vcols]
            ymix_ref[rows, sgu_w + hd * V_DIM:sgu_w + (hd + 1) * V_DIM] = hh.astype(jnp.bfloat16)
            dec = decay[hd:hd + 1, rows]
            state_ref[hd] = jnp.concatenate([dec, dec], axis=1) * st + local[c, hd]
        if (c + 1) % OUT_CHUNKS == 0:
            done = slice((c + 1 - OUT_CHUNKS) * MIX_CHUNK, (c + 1) * MIX_CHUNK)
            o_ref[done, :] = x_ref[done, :] + jnp.dot(
                ymix_ref[done, :], wout_ref[...], preferred_element_type=jnp.float32)


def _mixer(x, gain, w_in, ln_g, ln_b, sgu_w, sgu_bias, conv_w, conv_b, gate_b, mh, w_out):
    bsz, seq, d = x.shape
    rows_t = MIX_ROWS
    assert seq % rows_t == 0 and rows_t % MIX_CHUNK == 0
    tok_spec = pl.BlockSpec((None, rows_t, d), lambda b, s: (b, s, 0))
    ins = (gain, w_in, ln_g, ln_b, sgu_w, sgu_bias, conv_w, conv_b, gate_b, mh, w_out)
    qk_w = 2 * HEADS * QK_DIM
    mv_w = HEADS * V_DIM
    return pl.pallas_call(
        _mixer_kernel,
        grid=(bsz, seq // rows_t),
        in_specs=[tok_spec] + [_resident(a.shape) for a in ins],
        out_specs=tok_spec,
        out_shape=jax.ShapeDtypeStruct(x.shape, jnp.float32),
        scratch_shapes=[
            pltpu.VMEM((HEADS, LANES, 2 * V_DIM), jnp.float32),
            pltpu.VMEM((SUBLANES, rows_t), jnp.float32),
            pltpu.VMEM((qk_w // LANES, rows_t + SUBLANES, LANES), jnp.float32),
            pltpu.VMEM((rows_t, qk_w), jnp.float32),
            pltpu.VMEM((rows_t, mv_w), jnp.bfloat16),
            pltpu.VMEM((rows_t, mv_w), jnp.float32),
            pltpu.VMEM((rows_t, 2 * mv_w), jnp.bfloat16),
        ],
        compiler_params=pltpu.CompilerParams(
            dimension_semantics=("arbitrary", "arbitrary"), vmem_limit_bytes=VMEM_LIMIT_BYTES),
        name="mixer",
    )(x, *ins)


def kernel(x, ffn1_norm, ffn1_w_gate, ffn1_w_up, ffn1_w_down, mix_norm, w_in, sgu_ln_g, sgu_ln_b, sgu_w, sgu_b, conv_w, conv_b, igate_b, fgate_b, mh_norm, w_out, ffn2_norm, ffn2_w_gate, ffn2_w_up, ffn2_w_down, final_norm):
    bsz, seq, d = x.shape
    bf = jnp.bfloat16
    row = lambda a: a.reshape(1, -1)
    fin = row(final_norm)
    for l in range(ffn1_norm.shape[0]):
        x = _ffn(x.reshape(bsz * seq, d), row(ffn1_norm[l]), ffn1_w_gate[l].astype(bf),
                 ffn1_w_up[l].astype(bf), ffn1_w_down[l].astype(bf), fin, False).reshape(bsz, seq, d)
        n_main = w_in.shape[2] - 2 * HEADS
        w_in_l = jnp.pad(w_in[l], ((0, 0), (0, LANES - 2 * HEADS))).astype(bf)
        assert w_in_l.shape[1] == n_main + LANES
        sgu_bias = jnp.repeat(sgu_b[l].T, LANES, axis=1)
        gate_b = jnp.broadcast_to(
            jnp.concatenate([igate_b[l], fgate_b[l]])[:, None], (2 * HEADS, MIX_ROWS))
        x = _mixer(x, row(mix_norm[l]), w_in_l, row(sgu_ln_g[l]), row(sgu_ln_b[l]), sgu_w[l], sgu_bias,
                   conv_w[l], row(conv_b[l]), gate_b, row(mh_norm[l]), w_out[l].astype(bf))
        last = l == ffn1_norm.shape[0] - 1
        x = _ffn(x.reshape(bsz * seq, d), row(ffn2_norm[l]), ffn2_w_gate[l].astype(bf),
                 ffn2_w_up[l].astype(bf), ffn2_w_down[l].astype(bf), fin, last).reshape(bsz, seq, d)
    return x
```

```python
import functools

import jax
import jax.numpy as jnp
from jax import lax
from jax.experimental import pallas as pl
from jax.experimental.pallas import tpu as pltpu

EPS = 1e-6
LANES = 128
SUBLANES = 8
VMEM_LIMIT_BYTES = 56 * 1024 * 1024

STREAM_CHUNK = 64
SGU_CHUNK = 128
SGU_GROUPS = 4
HEADS = 4
QK_DIM = 64
V_DIM = 128
MIX_CHUNK = 128

FFN_ROWS = 1024
FFN_COLS = 256
MIX_ROWS = 1024
OUT_CHUNKS = 4
LOG2E = 1.4426950408889634


def _resident(shape):
    zeros = (0,) * len(shape)
    return pl.BlockSpec(shape, lambda *_: zeros, pipeline_mode=pl.Buffered(1))


def _rms(x, gain):
    return x * lax.rsqrt(jnp.mean(x * x, axis=-1, keepdims=True) + EPS) * gain


def _ffn_kernel(x_ref, gain_ref, wg_ref, wu_ref, wd_ref, fin_ref, o_ref, act_ref, *, final_norm):
    x = x_ref[...]
    h = _rms(x, gain_ref[...]).astype(jnp.bfloat16)
    d_ff = wg_ref.shape[1]
    for c in range(d_ff // FFN_COLS):
        cols = slice(c * FFN_COLS, (c + 1) * FFN_COLS)
        g = jnp.dot(h, wg_ref[:, cols], preferred_element_type=jnp.float32)
        u = jnp.dot(h, wu_ref[:, cols], preferred_element_type=jnp.float32)
        act_ref[:, cols] = (g * jax.nn.sigmoid(g) * u).astype(jnp.bfloat16)
    y = jnp.dot(act_ref[...], wd_ref[...], preferred_element_type=jnp.float32)
    out = x + 0.5 * y
    if final_norm:
        out = _rms(out, fin_ref[...])
    o_ref[...] = out


def _ffn(x2d, gain, wg, wu, wd, fin, final_norm):
    n, d = x2d.shape
    d_ff = wg.shape[1]
    assert n % FFN_ROWS == 0 and d_ff % FFN_COLS == 0
    row_spec = pl.BlockSpec((FFN_ROWS, d), lambda i: (i, 0))
    return pl.pallas_call(
        functools.partial(_ffn_kernel, final_norm=final_norm),
        grid=(n // FFN_ROWS,),
        in_specs=[row_spec, _resident((1, d)), _resident((d, d_ff)), _resident((d, d_ff)),
                  _resident((d_ff, d)), _resident((1, d))],
        out_specs=row_spec,
        out_shape=jax.ShapeDtypeStruct((n, d), jnp.float32),
        scratch_shapes=[pltpu.VMEM((FFN_ROWS, d_ff), jnp.bfloat16)],
        compiler_params=pltpu.CompilerParams(
            dimension_semantics=("arbitrary",), vmem_limit_bytes=VMEM_LIMIT_BYTES),
        name="ffn_final" if final_norm else "ffn",
    )(x2d, gain, wg, wu, wd, fin)


def _scan_lanes(x, combine, fill):
    width = x.shape[1]
    lane = lax.broadcasted_iota(jnp.int32, x.shape, 1)
    d = 1
    while d < width:
        x = combine(x, jnp.where(lane >= d, pltpu.roll(x, d, 1), fill))
        d *= 2
    return x


def _mixer_kernel(x_ref, gain_ref, win_ref, wgate_ref, lng_ref, lnb_ref, sw_ref, sb_ref, cw_ref, cb_ref,
                  gb_ref, mh_ref, wout_ref, o_ref,
                  state_ref, mcar_ref, ext_ref, qk_ref, v_ref, og_ref, ymix_ref):
    rows_t = x_ref.shape[0]
    n_chunks = rows_t // MIX_CHUNK
    sgu_w = SGU_GROUPS * LANES
    qk_w = 2 * HEADS * QK_DIM
    mv_w = HEADS * V_DIM
    o_uv, o_qk = 0, 2 * sgu_w
    o_v = o_qk + qk_w
    o_o = o_v + mv_w

    @pl.when(pl.program_id(1) == 0)
    def _():
        state_ref[...] = jnp.zeros_like(state_ref)
        mcar_ref[...] = jnp.zeros_like(mcar_ref)
        ext_ref[:, 0:SUBLANES, :] = jnp.zeros((qk_w // LANES, SUBLANES, LANES), jnp.float32)

    x = x_ref[...]
    h = _rms(x, gain_ref[...]).astype(jnp.bfloat16)

    def proj(lo, width):
        return jnp.dot(h, win_ref[:, lo:lo + width], preferred_element_type=jnp.float32)

    z_gate = jnp.dot(h, wgate_ref[...], preferred_element_type=jnp.float32)
    z_qk = proj(o_qk, qk_w)
    for t in range(qk_w // LANES):
        ext_ref[t, SUBLANES:, :] = z_qk[:, t * LANES:(t + 1) * LANES]
    uv = jax.nn.gelu(proj(o_uv, 2 * sgu_w))
    v_ref[...] = proj(o_v, mv_w).astype(jnp.bfloat16)
    og_ref[...] = jax.nn.sigmoid(proj(o_o, mv_w))

    u = uv[:, :sgu_w]
    v = uv[:, sgu_w:]
    vc = v - jnp.mean(v, axis=-1, keepdims=True)
    vn = vc * lax.rsqrt(jnp.mean(vc * vc, axis=-1, keepdims=True) + EPS) * lng_ref[...] + lnb_ref[...]
    vn = vn.astype(jnp.bfloat16)
    t_blk = lax.broadcasted_iota(jnp.int32, (SGU_CHUNK, SGU_CHUNK), 0) // STREAM_CHUNK
    s_blk = lax.broadcasted_iota(jnp.int32, (SGU_CHUNK, SGU_CHUNK), 1) // STREAM_CHUNK

    def spatial_gating():
        for g in range(SGU_GROUPS):
            cols = slice(g * LANES, (g + 1) * LANES)
            wm = jnp.where(t_blk >= s_blk, sw_ref[g], 0.0).astype(jnp.bfloat16)
            rhs = jnp.concatenate(
                [vn[c * SGU_CHUNK:(c + 1) * SGU_CHUNK, cols] for c in range(n_chunks)], axis=1)
            mixed = jnp.dot(wm, rhs, preferred_element_type=jnp.float32)
            for c in range(n_chunks):
                rows = slice(c * SGU_CHUNK, (c + 1) * SGU_CHUNK)
                gated = u[rows, cols] * (mixed[:, c * LANES:(c + 1) * LANES] + sb_ref[:, cols])
                ymix_ref[rows, cols] = gated.astype(jnp.bfloat16)

    for t in range(qk_w // LANES):
        cols = slice(t * LANES, (t + 1) * LANES)
        conv = cb_ref[:, cols]
        for j in range(cw_ref.shape[0]):
            conv = conv + cw_ref[j:j + 1, cols] * ext_ref[t, pl.ds(SUBLANES - 3 + j, rows_t), :]
        qk_ref[:, cols] = conv * jax.nn.sigmoid(conv)
        ext_ref[t, 0:SUBLANES, :] = ext_ref[t, rows_t:rows_t + SUBLANES, :]

    gts = z_gate.T[0:SUBLANES, :] + gb_ref[...]
    logf = pltpu.roll(jax.nn.log_sigmoid(gts), HEADS, 0)
    b_cum = _scan_lanes(logf, jnp.add, 0.0)
    g_in = gts - b_cum
    mm = jnp.maximum(mcar_ref[...], _scan_lanes(g_in, jnp.maximum, -jnp.inf))
    mm_last = jnp.concatenate(
        [jnp.broadcast_to(mm[:, (c + 1) * MIX_CHUNK - 1:(c + 1) * MIX_CHUNK], (SUBLANES, MIX_CHUNK))
         for c in range(n_chunks)], axis=1)
    mm_prev = jnp.concatenate([mcar_ref[:, :MIX_CHUNK], mm_last[:, :rows_t - MIX_CHUNK]], axis=1)
    decay = jnp.exp(mm_prev - mm_last)
    p_last = jnp.exp(g_in - mm_last)
    m_end = b_cum + mm
    mcar_ref[...] = jnp.broadcast_to(m_end[:, rows_t - 1:rows_t], (SUBLANES, rows_t))
    g_in2 = g_in * LOG2E
    mm_prev2 = mm_prev * LOG2E
    cols_t = jnp.concatenate(
        [mm * -LOG2E, m_end * -LOG2E, jnp.zeros((LANES - 2 * SUBLANES, rows_t), jnp.float32)], axis=0).T

    lane = lax.broadcasted_iota(jnp.int32, (1, LANES), 1)
    t_idx = lax.broadcasted_iota(jnp.int32, (MIX_CHUNK, MIX_CHUNK), 0)
    s_idx = lax.broadcasted_iota(jnp.int32, (MIX_CHUNK, MIX_CHUNK), 1)
    causal = s_idx <= t_idx
    ones_aug = jnp.ones((MIX_CHUNK, V_DIM), jnp.bfloat16)
    scores, local, vaugs, qs = {}, {}, {}, {}
    for c in range(n_chunks):
        rows = slice(c * MIX_CHUNK, (c + 1) * MIX_CHUNK)
        for hd in range(HEADS):
            pair = (hd // 2) * LANES
            in_head = (lane >= (hd % 2) * QK_DIM) & (lane < (hd % 2 + 1) * QK_DIM)
            k_scale = jnp.where(in_head, QK_DIM ** -0.5, 0.0)
            q = qk_ref[rows, pair:pair + LANES]
            k_t = (qk_ref[rows, HEADS * QK_DIM + pair:HEADS * QK_DIM + pair + LANES] * k_scale).T
            vcols = slice(hd * V_DIM, (hd + 1) * V_DIM)
            v_aug = jnp.concatenate([v_ref[rows, vcols], ones_aug], axis=1)
            scores[c, hd] = jnp.dot(q.astype(jnp.bfloat16), k_t.astype(jnp.bfloat16),
                                    preferred_element_type=jnp.float32)
            k_w = (k_t * p_last[hd:hd + 1, rows]).astype(jnp.bfloat16)
            local[c, hd] = jnp.dot(k_w, v_aug, preferred_element_type=jnp.float32)
            vaugs[c, hd] = v_aug
            qs[c, hd] = q
    spatial_gating()
    for c in range(n_chunks):
        rows = slice(c * MIX_CHUNK, (c + 1) * MIX_CHUNK)
        for hd in range(HEADS):
            vcols = slice(hd * V_DIM, (hd + 1) * V_DIM)
            neg_mm = jnp.broadcast_to(cols_t[rows, hd:hd + 1], (MIX_CHUNK, MIX_CHUNK))
            neg_m = jnp.broadcast_to(cols_t[rows, SUBLANES + hd:SUBLANES + hd + 1], (MIX_CHUNK, V_DIM))
            p = jnp.exp2(jnp.where(causal, g_in2[hd:hd + 1, rows] + neg_mm, -jnp.inf))
            w = (p * scores[c, hd]).astype(jnp.bfloat16)
            a_inter = jnp.exp2(mm_prev2[hd:hd + 1, rows] + neg_mm)
            st = state_ref[hd]
            lhs = jnp.concatenate([w, (qs[c, hd] * a_inter).astype(jnp.bfloat16)], axis=1)
            rhs = jnp.concatenate([vaugs[c, hd], st.astype(jnp.bfloat16)], axis=0)
            nd = jnp.dot(lhs, rhs, preferred_element_type=jnp.float32)
            num = nd[:, :V_DIM]
            den = nd[:, V_DIM:]
            hh = num / jnp.maximum(jnp.abs(den), jnp.exp2(neg_m))
            hh = hh * lax.rsqrt(jnp.mean(hh * hh, axis=-1, keepdims=True) + EPS)
            hh = hh * mh_ref[:, vcols] * og_ref[rows, vcols]
            ymix_ref[rows, sgu_w + hd * V_DIM:sgu_w + (hd + 1) * V_DIM] = hh.astype(jnp.bfloat16)
            dec = decay[hd:hd + 1, rows]
            state_ref[hd] = jnp.concatenate([dec, dec], axis=1) * st + local[c, hd]
        if (c + 1) % OUT_CHUNKS == 0:
            done = slice((c + 1 - OUT_CHUNKS) * MIX_CHUNK, (c + 1) * MIX_CHUNK)
            o_ref[done, :] = x_ref[done, :] + jnp.dot(
                ymix_ref[done, :], wout_ref[...], preferred_element_type=jnp.float32)


def _mixer(x, gain, w_in, w_gate, ln_g, ln_b, sgu_w, sgu_bias, conv_w, conv_b, gate_b, mh, w_out):
    bsz, seq, d = x.shape
    rows_t = MIX_ROWS
    assert seq % rows_t == 0 and rows_t % MIX_CHUNK == 0
    tok_spec = pl.BlockSpec((None, rows_t, d), lambda b, s: (b, s, 0))
    ins = (gain, w_in, w_gate, ln_g, ln_b, sgu_w, sgu_bias, conv_w, conv_b, gate_b, mh, w_out)
    qk_w = 2 * HEADS * QK_DIM
    mv_w = HEADS * V_DIM
    return pl.pallas_call(
        _mixer_kernel,
        grid=(bsz, seq // rows_t),
        in_specs=[tok_spec] + [_resident(a.shape) for a in ins],
        out_specs=tok_spec,
        out_shape=jax.ShapeDtypeStruct(x.shape, jnp.float32),
        scratch_shapes=[
            pltpu.VMEM((HEADS, LANES, 2 * V_DIM), jnp.float32),
            pltpu.VMEM((SUBLANES, rows_t), jnp.float32),
            pltpu.VMEM((qk_w // LANES, rows_t + SUBLANES, LANES), jnp.float32),
            pltpu.VMEM((rows_t, qk_w), jnp.float32),
            pltpu.VMEM((rows_t, mv_w), jnp.bfloat16),
            pltpu.VMEM((rows_t, mv_w), jnp.float32),
            pltpu.VMEM((rows_t, 2 * mv_w), jnp.bfloat16),
        ],
        compiler_params=pltpu.CompilerParams(
            dimension_semantics=("arbitrary", "arbitrary"), vmem_limit_bytes=VMEM_LIMIT_BYTES),
        name="mixer",
    )(x, *ins)


def kernel(x, ffn1_norm, ffn1_w_gate, ffn1_w_up, ffn1_w_down, mix_norm, w_in, sgu_ln_g, sgu_ln_b, sgu_w, sgu_b, conv_w, conv_b, igate_b, fgate_b, mh_norm, w_out, ffn2_norm, ffn2_w_gate, ffn2_w_up, ffn2_w_down, final_norm):
    bsz, seq, d = x.shape
    bf = jnp.bfloat16
    row = lambda a: a.reshape(1, -1)
    fin = row(final_norm)
    for l in range(ffn1_norm.shape[0]):
        x = _ffn(x.reshape(bsz * seq, d), row(ffn1_norm[l]), ffn1_w_gate[l].astype(bf),
                 ffn1_w_up[l].astype(bf), ffn1_w_down[l].astype(bf), fin, False).reshape(bsz, seq, d)
        n_main = w_in.shape[2] - 2 * HEADS
        w_main = w_in[l][:, :n_main].astype(bf)
        w_gate = jnp.pad(w_in[l][:, n_main:], ((0, 0), (0, LANES - 2 * HEADS))).astype(bf)
        sgu_bias = jnp.repeat(sgu_b[l].T, LANES, axis=1)
        gate_b = jnp.broadcast_to(
            jnp.concatenate([igate_b[l], fgate_b[l]])[:, None], (2 * HEADS, MIX_ROWS))
        x = _mixer(x, row(mix_norm[l]), w_main, w_gate, row(sgu_ln_g[l]), row(sgu_ln_b[l]), sgu_w[l], sgu_bias,
                   conv_w[l], row(conv_b[l]), gate_b, row(mh_norm[l]), w_out[l].astype(bf))
        last = l == ffn1_norm.shape[0] - 1
        x = _ffn(x.reshape(bsz * seq, d), row(ffn2_norm[l]), ffn2_w_gate[l].astype(bf),
                 ffn2_w_up[l].astype(bf), ffn2_w_down[l].astype(bf), fin, last).reshape(bsz, seq, d)
    return x
```

```python
import functools

import jax
import jax.numpy as jnp
from jax import lax
from jax.experimental import pallas as pl
from jax.experimental.pallas import tpu as pltpu

EPS = 1e-6
LANES = 128
SUBLANES = 8
VMEM_LIMIT_BYTES = 60 * 1024 * 1024

STREAM_CHUNK = 64
SGU_CHUNK = 128
SGU_GROUPS = 4
HEADS = 4
QK_DIM = 64
V_DIM = 128
MIX_CHUNK = 128

FFN_ROWS = 1024
FFN_COLS = 256
MIX_ROWS = 1024
OUT_CHUNKS = 4
LOG2E = 1.4426950408889634


def _resident(shape):
    zeros = (0,) * len(shape)
    return pl.BlockSpec(shape, lambda *_: zeros, pipeline_mode=pl.Buffered(1))


def _rms(x, gain):
    return x * lax.rsqrt(jnp.mean(x * x, axis=-1, keepdims=True) + EPS) * gain


def _ffn_kernel(x_ref, gain_ref, wg_ref, wu_ref, wd_ref, fin_ref, o_ref, act_ref, *, final_norm):
    x = x_ref[...]
    h = _rms(x, gain_ref[...]).astype(jnp.bfloat16)
    d_ff = wg_ref.shape[1]
    for c in range(d_ff // FFN_COLS):
        cols = slice(c * FFN_COLS, (c + 1) * FFN_COLS)
        g = jnp.dot(h, wg_ref[:, cols].astype(jnp.bfloat16), preferred_element_type=jnp.float32)
        u = jnp.dot(h, wu_ref[:, cols].astype(jnp.bfloat16), preferred_element_type=jnp.float32)
        act_ref[:, cols] = (g * jax.nn.sigmoid(g) * u).astype(jnp.bfloat16)
    y = jnp.dot(act_ref[...], wd_ref[...].astype(jnp.bfloat16), preferred_element_type=jnp.float32)
    out = x + 0.5 * y
    if final_norm:
        out = _rms(out, fin_ref[...])
    o_ref[...] = out


def _ffn(x2d, gain, wg, wu, wd, fin, final_norm):
    n, d = x2d.shape
    d_ff = wg.shape[1]
    assert n % FFN_ROWS == 0 and d_ff % FFN_COLS == 0
    row_spec = pl.BlockSpec((FFN_ROWS, d), lambda i: (i, 0))
    return pl.pallas_call(
        functools.partial(_ffn_kernel, final_norm=final_norm),
        grid=(n // FFN_ROWS,),
        in_specs=[row_spec, _resident((1, d)), _resident((d, d_ff)), _resident((d, d_ff)),
                  _resident((d_ff, d)), _resident((1, d))],
        out_specs=row_spec,
        out_shape=jax.ShapeDtypeStruct((n, d), jnp.float32),
        scratch_shapes=[pltpu.VMEM((FFN_ROWS, d_ff), jnp.bfloat16)],
        compiler_params=pltpu.CompilerParams(
            dimension_semantics=("arbitrary",), vmem_limit_bytes=VMEM_LIMIT_BYTES),
        name="ffn_final" if final_norm else "ffn",
    )(x2d, gain, wg, wu, wd, fin)


def _scan_lanes(x, combine, fill):
    width = x.shape[1]
    lane = lax.broadcasted_iota(jnp.int32, x.shape, 1)
    d = 1
    while d < width:
        x = combine(x, jnp.where(lane >= d, pltpu.roll(x, d, 1), fill))
        d *= 2
    return x


def _mixer_kernel(x_ref, gain_ref, win_ref, wgate_ref, lng_ref, lnb_ref, sw_ref, sb_ref, cw_ref, cb_ref,
                  gb_ref, mh_ref, wout_ref, o_ref,
                  state_ref, mcar_ref, ext_ref, qk_ref, v_ref, og_ref, ymix_ref):
    rows_t = x_ref.shape[0]
    n_chunks = rows_t // MIX_CHUNK
    sgu_w = SGU_GROUPS * LANES
    qk_w = 2 * HEADS * QK_DIM
    mv_w = HEADS * V_DIM
    o_uv, o_qk = 0, 2 * sgu_w
    o_v = o_qk + qk_w
    o_o = o_v + mv_w

    @pl.when(pl.program_id(1) == 0)
    def _():
        state_ref[...] = jnp.zeros_like(state_ref)
        mcar_ref[...] = jnp.zeros_like(mcar_ref)
        ext_ref[:, 0:SUBLANES, :] = jnp.zeros((qk_w // LANES, SUBLANES, LANES), jnp.float32)

    x = x_ref[...]
    h = _rms(x, gain_ref[...]).astype(jnp.bfloat16)

    def proj(lo, width):
        return jnp.dot(h, win_ref[:, lo:lo + width].astype(jnp.bfloat16), preferred_element_type=jnp.float32)

    z_gate = jnp.dot(h, wgate_ref[...].astype(jnp.bfloat16), preferred_element_type=jnp.float32)
    z_qk = proj(o_qk, qk_w)
    for t in range(qk_w // LANES):
        ext_ref[t, SUBLANES:, :] = z_qk[:, t * LANES:(t + 1) * LANES]
    uv = jax.nn.gelu(proj(o_uv, 2 * sgu_w))
    v_ref[...] = proj(o_v, mv_w).astype(jnp.bfloat16)
    og_ref[...] = jax.nn.sigmoid(proj(o_o, mv_w))

    u = uv[:, :sgu_w]
    v = uv[:, sgu_w:]
    vc = v - jnp.mean(v, axis=-1, keepdims=True)
    vn = vc * lax.rsqrt(jnp.mean(vc * vc, axis=-1, keepdims=True) + EPS) * lng_ref[...] + lnb_ref[...]
    vn = vn.astype(jnp.bfloat16)
    t_blk = lax.broadcasted_iota(jnp.int32, (SGU_CHUNK, SGU_CHUNK), 0) // STREAM_CHUNK
    s_blk = lax.broadcasted_iota(jnp.int32, (SGU_CHUNK, SGU_CHUNK), 1) // STREAM_CHUNK

    def spatial_gating():
        for g in range(SGU_GROUPS):
            cols = slice(g * LANES, (g + 1) * LANES)
            wm = jnp.where(t_blk >= s_blk, sw_ref[g], 0.0).astype(jnp.bfloat16)
            rhs = jnp.concatenate(
                [vn[c * SGU_CHUNK:(c + 1) * SGU_CHUNK, cols] for c in range(n_chunks)], axis=1)
            mixed = jnp.dot(wm, rhs, preferred_element_type=jnp.float32)
            for c in range(n_chunks):
                rows = slice(c * SGU_CHUNK, (c + 1) * SGU_CHUNK)
                gated = u[rows, cols] * (mixed[:, c * LANES:(c + 1) * LANES] + sb_ref[:, cols])
                ymix_ref[rows, cols] = gated.astype(jnp.bfloat16)

    for t in range(qk_w // LANES):
        cols = slice(t * LANES, (t + 1) * LANES)
        conv = cb_ref[:, cols]
        for j in range(cw_ref.shape[0]):
            conv = conv + cw_ref[j:j + 1, cols] * ext_ref[t, pl.ds(SUBLANES - 3 + j, rows_t), :]
        qk_ref[:, cols] = conv * jax.nn.sigmoid(conv)
        ext_ref[t, 0:SUBLANES, :] = ext_ref[t, rows_t:rows_t + SUBLANES, :]

    gts = z_gate.T[0:SUBLANES, :] + gb_ref[...]
    logf = pltpu.roll(jax.nn.log_sigmoid(gts), HEADS, 0)
    b_cum = _scan_lanes(logf, jnp.add, 0.0)
    g_in = gts - b_cum
    mm = jnp.maximum(mcar_ref[...], _scan_lanes(g_in, jnp.maximum, -jnp.inf))
    mm_last = jnp.concatenate(
        [jnp.broadcast_to(mm[:, (c + 1) * MIX_CHUNK - 1:(c + 1) * MIX_CHUNK], (SUBLANES, MIX_CHUNK))
         for c in range(n_chunks)], axis=1)
    mm_prev = jnp.concatenate([mcar_ref[:, :MIX_CHUNK], mm_last[:, :rows_t - MIX_CHUNK]], axis=1)
    decay = jnp.exp(mm_prev - mm_last)
    p_last = jnp.exp(g_in - mm_last)
    m_end = b_cum + mm
    mcar_ref[...] = jnp.broadcast_to(m_end[:, rows_t - 1:rows_t], (SUBLANES, rows_t))
    g_in2 = g_in * LOG2E
    mm_prev2 = mm_prev * LOG2E
    cols_t = jnp.concatenate(
        [mm * -LOG2E, m_end * -LOG2E, jnp.zeros((LANES - 2 * SUBLANES, rows_t), jnp.float32)], axis=0).T

    lane = lax.broadcasted_iota(jnp.int32, (1, LANES), 1)
    t_idx = lax.broadcasted_iota(jnp.int32, (MIX_CHUNK, MIX_CHUNK), 0)
    s_idx = lax.broadcasted_iota(jnp.int32, (MIX_CHUNK, MIX_CHUNK), 1)
    causal = s_idx <= t_idx
    ones_aug = jnp.ones((MIX_CHUNK, V_DIM), jnp.bfloat16)
    scores, local, vaugs, qs = {}, {}, {}, {}
    for c in range(n_chunks):
        rows = slice(c * MIX_CHUNK, (c + 1) * MIX_CHUNK)
        for hd in range(HEADS):
            pair = (hd // 2) * LANES
            in_head = (lane >= (hd % 2) * QK_DIM) & (lane < (hd % 2 + 1) * QK_DIM)
            k_scale = jnp.where(in_head, QK_DIM ** -0.5, 0.0)
            q = qk_ref[rows, pair:pair + LANES]
            k_t = (qk_ref[rows, HEADS * QK_DIM + pair:HEADS * QK_DIM + pair + LANES] * k_scale).T
            vcols = slice(hd * V_DIM, (hd + 1) * V_DIM)
            v_aug = jnp.concatenate([v_ref[rows, vcols], ones_aug], axis=1)
            scores[c, hd] = jnp.dot(q.astype(jnp.bfloat16), k_t.astype(jnp.bfloat16),
                                    preferred_element_type=jnp.float32)
            k_w = (k_t * p_last[hd:hd + 1, rows]).astype(jnp.bfloat16)
            local[c, hd] = jnp.dot(k_w, v_aug, preferred_element_type=jnp.float32)
            vaugs[c, hd] = v_aug
            qs[c, hd] = q
    spatial_gating()
    for c in range(n_chunks):
        rows = slice(c * MIX_CHUNK, (c + 1) * MIX_CHUNK)
        for hd in range(HEADS):
            vcols = slice(hd * V_DIM, (hd + 1) * V_DIM)
            neg_mm = jnp.broadcast_to(cols_t[rows, hd:hd + 1], (MIX_CHUNK, MIX_CHUNK))
            neg_m = jnp.broadcast_to(cols_t[rows, SUBLANES + hd:SUBLANES + hd + 1], (MIX_CHUNK, V_DIM))
            p = jnp.exp2(jnp.where(causal, g_in2[hd:hd + 1, rows] + neg_mm, -jnp.inf))
            w = (p * scores[c, hd]).astype(jnp.bfloat16)
            a_inter = jnp.exp2(mm_prev2[hd:hd + 1, rows] + neg_mm)
            st = state_ref[hd]
            lhs = jnp.concatenate([w, (qs[c, hd] * a_inter).astype(jnp.bfloat16)], axis=1)
            rhs = jnp.concatenate([vaugs[c, hd], st.astype(jnp.bfloat16)], axis=0)
            nd = jnp.dot(lhs, rhs, preferred_element_type=jnp.float32)
            num = nd[:, :V_DIM]
            den = nd[:, V_DIM:]
            hh = num / jnp.maximum(jnp.abs(den), jnp.exp2(neg_m))
            hh = hh * lax.rsqrt(jnp.mean(hh * hh, axis=-1, keepdims=True) + EPS)
            hh = hh * mh_ref[:, vcols] * og_ref[rows, vcols]
            ymix_ref[rows, sgu_w + hd * V_DIM:sgu_w + (hd + 1) * V_DIM] = hh.astype(jnp.bfloat16)
            dec = decay[hd:hd + 1, rows]
            state_ref[hd] = jnp.concatenate([dec, dec], axis=1) * st + local[c, hd]
        if (c + 1) % OUT_CHUNKS == 0:
            done = slice((c + 1 - OUT_CHUNKS) * MIX_CHUNK, (c + 1) * MIX_CHUNK)
            o_ref[done, :] = x_ref[done, :] + jnp.dot(
                ymix_ref[done, :], wout_ref[...].astype(jnp.bfloat16), preferred_element_type=jnp.float32)


def _mixer(x, gain, w_in, w_gate, ln_g, ln_b, sgu_w, sgu_bias, conv_w, conv_b, gate_b, mh, w_out):
    bsz, seq, d = x.shape
    rows_t = MIX_ROWS
    assert seq % rows_t == 0 and rows_t % MIX_CHUNK == 0
    tok_spec = pl.BlockSpec((None, rows_t, d), lambda b, s: (b, s, 0))
    ins = (gain, w_in, w_gate, ln_g, ln_b, sgu_w, sgu_bias, conv_w, conv_b, gate_b, mh, w_out)
    w_main_shape = (d, w_in.shape[1] - 2 * HEADS)
    qk_w = 2 * HEADS * QK_DIM
    mv_w = HEADS * V_DIM
    return pl.pallas_call(
        _mixer_kernel,
        grid=(bsz, seq // rows_t),
        in_specs=[tok_spec] + [_resident(w_main_shape if a is w_in else a.shape) for a in ins],
        out_specs=tok_spec,
        out_shape=jax.ShapeDtypeStruct(x.shape, jnp.float32),
        scratch_shapes=[
            pltpu.VMEM((HEADS, LANES, 2 * V_DIM), jnp.float32),
            pltpu.VMEM((SUBLANES, rows_t), jnp.float32),
            pltpu.VMEM((qk_w // LANES, rows_t + SUBLANES, LANES), jnp.float32),
            pltpu.VMEM((rows_t, qk_w), jnp.float32),
            pltpu.VMEM((rows_t, mv_w), jnp.bfloat16),
            pltpu.VMEM((rows_t, mv_w), jnp.float32),
            pltpu.VMEM((rows_t, 2 * mv_w), jnp.bfloat16),
        ],
        compiler_params=pltpu.CompilerParams(
            dimension_semantics=("arbitrary", "arbitrary"), vmem_limit_bytes=VMEM_LIMIT_BYTES),
        name="mixer",
    )(x, *ins)


def kernel(x, ffn1_norm, ffn1_w_gate, ffn1_w_up, ffn1_w_down, mix_norm, w_in, sgu_ln_g, sgu_ln_b, sgu_w, sgu_b, conv_w, conv_b, igate_b, fgate_b, mh_norm, w_out, ffn2_norm, ffn2_w_gate, ffn2_w_up, ffn2_w_down, final_norm):
    bsz, seq, d = x.shape
    row = lambda a: a.reshape(1, -1)
    fin = row(final_norm)
    for l in range(ffn1_norm.shape[0]):
        x = _ffn(x.reshape(bsz * seq, d), row(ffn1_norm[l]), ffn1_w_gate[l],
                 ffn1_w_up[l], ffn1_w_down[l], fin, False).reshape(bsz, seq, d)
        n_main = w_in.shape[2] - 2 * HEADS
        w_gate = jnp.pad(w_in[l][:, n_main:], ((0, 0), (0, LANES - 2 * HEADS)))
        sgu_bias = jnp.repeat(sgu_b[l].T, LANES, axis=1)
        gate_b = jnp.broadcast_to(
            jnp.concatenate([igate_b[l], fgate_b[l]])[:, None], (2 * HEADS, MIX_ROWS))
        x = _mixer(x, row(mix_norm[l]), w_in[l], w_gate, row(sgu_ln_g[l]), row(sgu_ln_b[l]), sgu_w[l], sgu_bias,
                   conv_w[l], row(conv_b[l]), gate_b, row(mh_norm[l]), w_out[l])
        last = l == ffn1_norm.shape[0] - 1
        x = _ffn(x.reshape(bsz * seq, d), row(ffn2_norm[l]), ffn2_w_gate[l],
                 ffn2_w_up[l], ffn2_w_down[l], fin, last).reshape(bsz, seq, d)
    return x
```

```python
import functools

import jax
import jax.numpy as jnp
from jax import lax
from jax.experimental import pallas as pl
from jax.experimental.pallas import tpu as pltpu

EPS = 1e-6
LANES = 128
SUBLANES = 8
VMEM_LIMIT_BYTES = 60 * 1024 * 1024

STREAM_CHUNK = 64
SGU_CHUNK = 128
SGU_GROUPS = 4
HEADS = 4
QK_DIM = 64
V_DIM = 128
MIX_CHUNK = 128

FFN_ROWS = 1024
DOWN_ROWS = 512
FFN_COLS = 256
MIX_ROWS = 1024
OUT_CHUNKS = 4
LOG2E = 1.4426950408889634


def _resident(shape):
    zeros = (0,) * len(shape)
    return pl.BlockSpec(shape, lambda *_: zeros, pipeline_mode=pl.Buffered(1))


def _rms(x, gain):
    return x * lax.rsqrt(jnp.mean(x * x, axis=-1, keepdims=True) + EPS) * gain


def _ffn_kernel(x_ref, gain_ref, wg_ref, wu_ref, wd_ref, fin_ref, o_ref, act_ref, *, final_norm):
    x = x_ref[...]
    h = _rms(x, gain_ref[...]).astype(jnp.bfloat16)
    d_ff = wg_ref.shape[1]
    for c in range(d_ff // FFN_COLS):
        cols = slice(c * FFN_COLS, (c + 1) * FFN_COLS)
        g = jnp.dot(h, wg_ref[:, cols].astype(jnp.bfloat16), preferred_element_type=jnp.float32)
        u = jnp.dot(h, wu_ref[:, cols].astype(jnp.bfloat16), preferred_element_type=jnp.float32)
        act_ref[:, cols] = (g * jax.nn.sigmoid(g) * u).astype(jnp.bfloat16)
    for r in range(0, x_ref.shape[0], DOWN_ROWS):
        rows = slice(r, r + DOWN_ROWS)
        y = jnp.dot(act_ref[rows, :], wd_ref[...].astype(jnp.bfloat16), preferred_element_type=jnp.float32)
        out = x_ref[rows, :] + 0.5 * y
        if final_norm:
            out = _rms(out, fin_ref[...])
        o_ref[rows, :] = out


def _ffn(x2d, gain, wg, wu, wd, fin, final_norm):
    n, d = x2d.shape
    d_ff = wg.shape[1]
    assert n % FFN_ROWS == 0 and d_ff % FFN_COLS == 0
    row_spec = pl.BlockSpec((FFN_ROWS, d), lambda i: (i, 0))
    return pl.pallas_call(
        functools.partial(_ffn_kernel, final_norm=final_norm),
        grid=(n // FFN_ROWS,),
        in_specs=[row_spec, _resident((1, d)), _resident((d, d_ff)), _resident((d, d_ff)),
                  _resident((d_ff, d)), _resident((1, d))],
        out_specs=row_spec,
        out_shape=jax.ShapeDtypeStruct((n, d), jnp.float32),
        scratch_shapes=[pltpu.VMEM((FFN_ROWS, d_ff), jnp.bfloat16)],
        compiler_params=pltpu.CompilerParams(
            dimension_semantics=("arbitrary",), vmem_limit_bytes=VMEM_LIMIT_BYTES),
        name="ffn_final" if final_norm else "ffn",
    )(x2d, gain, wg, wu, wd, fin)


def _scan_lanes(x, combine, fill):
    width = x.shape[1]
    lane = lax.broadcasted_iota(jnp.int32, x.shape, 1)
    d = 1
    while d < width:
        x = combine(x, jnp.where(lane >= d, pltpu.roll(x, d, 1), fill))
        d *= 2
    return x


def _mixer_kernel(x_ref, gain_ref, win_ref, wgate_ref, lng_ref, lnb_ref, sw_ref, sb_ref, cw_ref, cb_ref,
                  gb_ref, mh_ref, wout_ref, o_ref,
                  state_ref, mcar_ref, ext_ref, qk_ref, v_ref, og_ref, ymix_ref, win_bf, wgate_bf, wout_bf):
    rows_t = x_ref.shape[0]
    n_chunks = rows_t // MIX_CHUNK
    sgu_w = SGU_GROUPS * LANES
    qk_w = 2 * HEADS * QK_DIM
    mv_w = HEADS * V_DIM
    o_uv, o_qk = 0, 2 * sgu_w
    o_v = o_qk + qk_w
    o_o = o_v + mv_w

    @pl.when((pl.program_id(0) == 0) & (pl.program_id(1) == 0))
    def _():
        win_bf[...] = win_ref[...].astype(jnp.bfloat16)
        wgate_bf[...] = wgate_ref[...].astype(jnp.bfloat16)
        wout_bf[...] = wout_ref[...].astype(jnp.bfloat16)

    @pl.when(pl.program_id(1) == 0)
    def _():
        state_ref[...] = jnp.zeros_like(state_ref)
        mcar_ref[...] = jnp.zeros_like(mcar_ref)
        ext_ref[:, 0:SUBLANES, :] = jnp.zeros((qk_w // LANES, SUBLANES, LANES), jnp.float32)

    x = x_ref[...]
    h = _rms(x, gain_ref[...]).astype(jnp.bfloat16)

    def proj(lo, width):
        return jnp.dot(h, win_bf[:, lo:lo + width], preferred_element_type=jnp.float32)

    z_gate = jnp.dot(h, wgate_bf[...], preferred_element_type=jnp.float32)
    z_qk = proj(o_qk, qk_w)
    for t in range(qk_w // LANES):
        ext_ref[t, SUBLANES:, :] = z_qk[:, t * LANES:(t + 1) * LANES]
    uv = jax.nn.gelu(proj(o_uv, 2 * sgu_w))
    v_ref[...] = proj(o_v, mv_w).astype(jnp.bfloat16)
    og_ref[...] = jax.nn.sigmoid(proj(o_o, mv_w))

    u = uv[:, :sgu_w]
    v = uv[:, sgu_w:]
    vc = v - jnp.mean(v, axis=-1, keepdims=True)
    vn = vc * lax.rsqrt(jnp.mean(vc * vc, axis=-1, keepdims=True) + EPS) * lng_ref[...] + lnb_ref[...]
    vn = vn.astype(jnp.bfloat16)
    t_blk = lax.broadcasted_iota(jnp.int32, (SGU_CHUNK, SGU_CHUNK), 0) // STREAM_CHUNK
    s_blk = lax.broadcasted_iota(jnp.int32, (SGU_CHUNK, SGU_CHUNK), 1) // STREAM_CHUNK

    def spatial_gating():
        for g in range(SGU_GROUPS):
            cols = slice(g * LANES, (g + 1) * LANES)
            wm = jnp.where(t_blk >= s_blk, sw_ref[g], 0.0).astype(jnp.bfloat16)
            rhs = jnp.concatenate(
                [vn[c * SGU_CHUNK:(c + 1) * SGU_CHUNK, cols] for c in range(n_chunks)], axis=1)
            mixed = jnp.dot(wm, rhs, preferred_element_type=jnp.float32)
            for c in range(n_chunks):
                rows = slice(c * SGU_CHUNK, (c + 1) * SGU_CHUNK)
                gated = u[rows, cols] * (mixed[:, c * LANES:(c + 1) * LANES] + sb_ref[:, cols])
                ymix_ref[rows, cols] = gated.astype(jnp.bfloat16)

    for t in range(qk_w // LANES):
        cols = slice(t * LANES, (t + 1) * LANES)
        conv = cb_ref[:, cols]
        for j in range(cw_ref.shape[0]):
            conv = conv + cw_ref[j:j + 1, cols] * ext_ref[t, pl.ds(SUBLANES - 3 + j, rows_t), :]
        qk_ref[:, cols] = conv * jax.nn.sigmoid(conv)
        ext_ref[t, 0:SUBLANES, :] = ext_ref[t, rows_t:rows_t + SUBLANES, :]

    gts = z_gate.T[0:SUBLANES, :] + gb_ref[...]
    logf = pltpu.roll(jax.nn.log_sigmoid(gts), HEADS, 0)
    b_cum = _scan_lanes(logf, jnp.add, 0.0)
    g_in = gts - b_cum
    mm = jnp.maximum(mcar_ref[...], _scan_lanes(g_in, jnp.maximum, -jnp.inf))
    mm_last = jnp.concatenate(
        [jnp.broadcast_to(mm[:, (c + 1) * MIX_CHUNK - 1:(c + 1) * MIX_CHUNK], (SUBLANES, MIX_CHUNK))
         for c in range(n_chunks)], axis=1)
    mm_prev = jnp.concatenate([mcar_ref[:, :MIX_CHUNK], mm_last[:, :rows_t - MIX_CHUNK]], axis=1)
    decay = jnp.exp(mm_prev - mm_last)
    p_last = jnp.exp(g_in - mm_last)
    m_end = b_cum + mm
    mcar_ref[...] = jnp.broadcast_to(m_end[:, rows_t - 1:rows_t], (SUBLANES, rows_t))
    g_in2 = g_in * LOG2E
    mm_prev2 = mm_prev * LOG2E
    cols_t = jnp.concatenate(
        [mm * -LOG2E, m_end * -LOG2E, jnp.zeros((LANES - 2 * SUBLANES, rows_t), jnp.float32)], axis=0).T

    lane = lax.broadcasted_iota(jnp.int32, (1, LANES), 1)
    t_idx = lax.broadcasted_iota(jnp.int32, (MIX_CHUNK, MIX_CHUNK), 0)
    s_idx = lax.broadcasted_iota(jnp.int32, (MIX_CHUNK, MIX_CHUNK), 1)
    causal = s_idx <= t_idx
    ones_aug = jnp.ones((MIX_CHUNK, V_DIM), jnp.bfloat16)
    scores, local, vaugs, qs = {}, {}, {}, {}
    for c in range(n_chunks):
        rows = slice(c * MIX_CHUNK, (c + 1) * MIX_CHUNK)
        for hd in range(HEADS):
            pair = (hd // 2) * LANES
            in_head = (lane >= (hd % 2) * QK_DIM) & (lane < (hd % 2 + 1) * QK_DIM)
            k_scale = jnp.where(in_head, QK_DIM ** -0.5, 0.0)
            q = qk_ref[rows, pair:pair + LANES]
            k_t = (qk_ref[rows, HEADS * QK_DIM + pair:HEADS * QK_DIM + pair + LANES] * k_scale).T
            vcols = slice(hd * V_DIM, (hd + 1) * V_DIM)
            v_aug = jnp.concatenate([v_ref[rows, vcols], ones_aug], axis=1)
            scores[c, hd] = jnp.dot(q.astype(jnp.bfloat16), k_t.astype(jnp.bfloat16),
                                    preferred_element_type=jnp.float32)
            k_w = (k_t * p_last[hd:hd + 1, rows]).astype(jnp.bfloat16)
            local[c, hd] = jnp.dot(k_w, v_aug, preferred_element_type=jnp.float32)
            vaugs[c, hd] = v_aug
            qs[c, hd] = q
    spatial_gating()
    for c in range(n_chunks):
        rows = slice(c * MIX_CHUNK, (c + 1) * MIX_CHUNK)
        for hd in range(HEADS):
            vcols = slice(hd * V_DIM, (hd + 1) * V_DIM)
            neg_mm = jnp.broadcast_to(cols_t[rows, hd:hd + 1], (MIX_CHUNK, MIX_CHUNK))
            neg_m = jnp.broadcast_to(cols_t[rows, SUBLANES + hd:SUBLANES + hd + 1], (MIX_CHUNK, V_DIM))
            p = jnp.exp2(jnp.where(causal, g_in2[hd:hd + 1, rows] + neg_mm, -jnp.inf))
            w = (p * scores[c, hd]).astype(jnp.bfloat16)
            a_inter = jnp.exp2(mm_prev2[hd:hd + 1, rows] + neg_mm)
            st = state_ref[hd]
            lhs = jnp.concatenate([w, (qs[c, hd] * a_inter).astype(jnp.bfloat16)], axis=1)
            rhs = jnp.concatenate([vaugs[c, hd], st.astype(jnp.bfloat16)], axis=0)
            nd = jnp.dot(lhs, rhs, preferred_element_type=jnp.float32)
            num = nd[:, :V_DIM]
            den = nd[:, V_DIM:]
            hh = num / jnp.maximum(jnp.abs(den), jnp.exp2(neg_m))
            hh = hh * lax.rsqrt(jnp.mean(hh * hh, axis=-1, keepdims=True) + EPS)
            hh = hh * mh_ref[:, vcols] * og_ref[rows, vcols]
            ymix_ref[rows, sgu_w + hd * V_DIM:sgu_w + (hd + 1) * V_DIM] = hh.astype(jnp.bfloat16)
            dec = decay[hd:hd + 1, rows]
            state_ref[hd] = jnp.concatenate([dec, dec], axis=1) * st + local[c, hd]
        if (c + 1) % OUT_CHUNKS == 0:
            done = slice((c + 1 - OUT_CHUNKS) * MIX_CHUNK, (c + 1) * MIX_CHUNK)
            o_ref[done, :] = x_ref[done, :] + jnp.dot(
                ymix_ref[done, :], wout_bf[...], preferred_element_type=jnp.float32)


def _mixer(x, gain, w_in, w_gate, ln_g, ln_b, sgu_w, sgu_bias, conv_w, conv_b, gate_b, mh, w_out):
    bsz, seq, d = x.shape
    rows_t = MIX_ROWS
    assert seq % rows_t == 0 and rows_t % MIX_CHUNK == 0
    tok_spec = pl.BlockSpec((None, rows_t, d), lambda b, s: (b, s, 0))
    ins = (gain, w_in, w_gate, ln_g, ln_b, sgu_w, sgu_bias, conv_w, conv_b, gate_b, mh, w_out)
    w_main_shape = (d, w_in.shape[1] - 2 * HEADS)
    qk_w = 2 * HEADS * QK_DIM
    mv_w = HEADS * V_DIM
    return pl.pallas_call(
        _mixer_kernel,
        grid=(bsz, seq // rows_t),
        in_specs=[tok_spec] + [_resident(w_main_shape if a is w_in else a.shape) for a in ins],
        out_specs=tok_spec,
        out_shape=jax.ShapeDtypeStruct(x.shape, jnp.float32),
        scratch_shapes=[
            pltpu.VMEM((HEADS, LANES, 2 * V_DIM), jnp.float32),
            pltpu.VMEM((SUBLANES, rows_t), jnp.float32),
            pltpu.VMEM((qk_w // LANES, rows_t + SUBLANES, LANES), jnp.float32),
            pltpu.VMEM((rows_t, qk_w), jnp.float32),
            pltpu.VMEM((rows_t, mv_w), jnp.bfloat16),
            pltpu.VMEM((rows_t, mv_w), jnp.float32),
            pltpu.VMEM((rows_t, 2 * mv_w), jnp.bfloat16),
            pltpu.VMEM(w_main_shape, jnp.bfloat16),
            pltpu.VMEM(w_gate.shape, jnp.bfloat16),
            pltpu.VMEM(w_out.shape, jnp.bfloat16),
        ],
        compiler_params=pltpu.CompilerParams(
            dimension_semantics=("arbitrary", "arbitrary"), vmem_limit_bytes=VMEM_LIMIT_BYTES),
        name="mixer",
    )(x, *ins)


def kernel(x, ffn1_norm, ffn1_w_gate, ffn1_w_up, ffn1_w_down, mix_norm, w_in, sgu_ln_g, sgu_ln_b, sgu_w, sgu_b, conv_w, conv_b, igate_b, fgate_b, mh_norm, w_out, ffn2_norm, ffn2_w_gate, ffn2_w_up, ffn2_w_down, final_norm):
    bsz, seq, d = x.shape
    row = lambda a: a.reshape(1, -1)
    fin = row(final_norm)
    for l in range(ffn1_norm.shape[0]):
        x = _ffn(x.reshape(bsz * seq, d), row(ffn1_norm[l]), ffn1_w_gate[l],
                 ffn1_w_up[l], ffn1_w_down[l], fin, False).reshape(bsz, seq, d)
        n_main = w_in.shape[2] - 2 * HEADS
        w_gate = jnp.pad(w_in[l][:, n_main:], ((0, 0), (0, LANES - 2 * HEADS)))
        sgu_bias = jnp.repeat(sgu_b[l].T, LANES, axis=1)
        gate_b = jnp.broadcast_to(
            jnp.concatenate([igate_b[l], fgate_b[l]])[:, None], (2 * HEADS, MIX_ROWS))
        x = _mixer(x, row(mix_norm[l]), w_in[l], w_gate, row(sgu_ln_g[l]), row(sgu_ln_b[l]), sgu_w[l], sgu_bias,
                   conv_w[l], row(conv_b[l]), gate_b, row(mh_norm[l]), w_out[l])
        last = l == ffn1_norm.shape[0] - 1
        x = _ffn(x.reshape(bsz * seq, d), row(ffn2_norm[l]), ffn2_w_gate[l],
                 ffn2_w_up[l], ffn2_w_down[l], fin, last).reshape(bsz, seq, d)
    return x
```

```python
import functools

import jax
import jax.numpy as jnp
from jax import lax
from jax.experimental import pallas as pl
from jax.experimental.pallas import tpu as pltpu

EPS = 1e-6
LANES = 128
SUBLANES = 8
VMEM_LIMIT_BYTES = 60 * 1024 * 1024

STREAM_CHUNK = 64
SGU_CHUNK = 128
SGU_GROUPS = 4
HEADS = 4
QK_DIM = 64
V_DIM = 128
MIX_CHUNK = 128

FFN_ROWS = 1024
DOWN_ROWS = 512
FFN_COLS = 256
MIX_ROWS = 1024
OUT_CHUNKS = 4
LOG2E = 1.4426950408889634


def _resident(shape):
    zeros = (0,) * len(shape)
    return pl.BlockSpec(shape, lambda *_: zeros, pipeline_mode=pl.Buffered(1))


def _rms(x, gain):
    return x * lax.rsqrt(jnp.mean(x * x, axis=-1, keepdims=True) + EPS) * gain


def _ffn_kernel(x_ref, gain_ref, wg_ref, wu_ref, wd_ref, fin_ref, o_ref, act_ref, *, final_norm):
    x = x_ref[...]
    h = (x * gain_ref[...]).astype(jnp.bfloat16)
    rs = lax.rsqrt(jnp.mean(x * x, axis=-1, keepdims=True) + EPS)
    d_ff = wg_ref.shape[1]
    for c in range(d_ff // FFN_COLS):
        cols = slice(c * FFN_COLS, (c + 1) * FFN_COLS)
        g = rs * jnp.dot(h, wg_ref[:, cols].astype(jnp.bfloat16), preferred_element_type=jnp.float32)
        u = rs * jnp.dot(h, wu_ref[:, cols].astype(jnp.bfloat16), preferred_element_type=jnp.float32)
        act_ref[:, cols] = (g * jax.nn.sigmoid(g) * u).astype(jnp.bfloat16)
    for r in range(0, x_ref.shape[0], DOWN_ROWS):
        rows = slice(r, r + DOWN_ROWS)
        y = jnp.dot(act_ref[rows, :], wd_ref[...].astype(jnp.bfloat16), preferred_element_type=jnp.float32)
        out = x_ref[rows, :] + 0.5 * y
        if final_norm:
            out = _rms(out, fin_ref[...])
        o_ref[rows, :] = out


def _ffn(x2d, gain, wg, wu, wd, fin, final_norm):
    n, d = x2d.shape
    d_ff = wg.shape[1]
    assert n % FFN_ROWS == 0 and d_ff % FFN_COLS == 0
    row_spec = pl.BlockSpec((FFN_ROWS, d), lambda i: (i, 0))
    return pl.pallas_call(
        functools.partial(_ffn_kernel, final_norm=final_norm),
        grid=(n // FFN_ROWS,),
        in_specs=[row_spec, _resident((1, d)), _resident((d, d_ff)), _resident((d, d_ff)),
                  _resident((d_ff, d)), _resident((1, d))],
        out_specs=row_spec,
        out_shape=jax.ShapeDtypeStruct((n, d), jnp.float32),
        scratch_shapes=[pltpu.VMEM((FFN_ROWS, d_ff), jnp.bfloat16)],
        compiler_params=pltpu.CompilerParams(
            dimension_semantics=("arbitrary",), vmem_limit_bytes=VMEM_LIMIT_BYTES),
        name="ffn_final" if final_norm else "ffn",
    )(x2d, gain, wg, wu, wd, fin)


def _scan_lanes(x, combine, fill):
    width = x.shape[1]
    lane = lax.broadcasted_iota(jnp.int32, x.shape, 1)
    d = 1
    while d < width:
        x = combine(x, jnp.where(lane >= d, pltpu.roll(x, d, 1), fill))
        d *= 2
    return x


def _mixer_kernel(x_ref, gain_ref, win_ref, wgate_ref, lng_ref, lnb_ref, sw_ref, sb_ref, cw_ref, cb_ref,
                  gb_ref, mh_ref, wout_ref, o_ref,
                  state_ref, mcar_ref, ext_ref, qk_ref, v_ref, og_ref, ymix_ref, win_bf, wgate_bf, wout_bf):
    rows_t = x_ref.shape[0]
    n_chunks = rows_t // MIX_CHUNK
    sgu_w = SGU_GROUPS * LANES
    qk_w = 2 * HEADS * QK_DIM
    mv_w = HEADS * V_DIM
    o_uv, o_qk = 0, 2 * sgu_w
    o_v = o_qk + qk_w
    o_o = o_v + mv_w

    @pl.when((pl.program_id(0) == 0) & (pl.program_id(1) == 0))
    def _():
        win_bf[...] = win_ref[...].astype(jnp.bfloat16)
        wgate_bf[...] = wgate_ref[...].astype(jnp.bfloat16)
        wout_bf[...] = wout_ref[...].astype(jnp.bfloat16)

    @pl.when(pl.program_id(1) == 0)
    def _():
        state_ref[...] = jnp.zeros_like(state_ref)
        mcar_ref[...] = jnp.zeros_like(mcar_ref)
        ext_ref[:, 0:SUBLANES, :] = jnp.zeros((qk_w // LANES, SUBLANES, LANES), jnp.float32)

    x = x_ref[...]
    h = _rms(x, gain_ref[...]).astype(jnp.bfloat16)

    def proj(lo, width):
        return jnp.dot(h, win_bf[:, lo:lo + width], preferred_element_type=jnp.float32)

    z_gate = jnp.dot(h, wgate_bf[...], preferred_element_type=jnp.float32)
    z_qk = proj(o_qk, qk_w)
    for t in range(qk_w // LANES):
        ext_ref[t, SUBLANES:, :] = z_qk[:, t * LANES:(t + 1) * LANES]
    uv = jax.nn.gelu(proj(o_uv, 2 * sgu_w))
    v_ref[...] = proj(o_v, mv_w).astype(jnp.bfloat16)
    og_ref[...] = jax.nn.sigmoid(proj(o_o, mv_w))

    u = uv[:, :sgu_w]
    v = uv[:, sgu_w:]
    vc = v - jnp.mean(v, axis=-1, keepdims=True)
    vn = vc * lax.rsqrt(jnp.mean(vc * vc, axis=-1, keepdims=True) + EPS) * lng_ref[...] + lnb_ref[...]
    vn = vn.astype(jnp.bfloat16)
    t_blk = lax.broadcasted_iota(jnp.int32, (SGU_CHUNK, SGU_CHUNK), 0) // STREAM_CHUNK
    s_blk = lax.broadcasted_iota(jnp.int32, (SGU_CHUNK, SGU_CHUNK), 1) // STREAM_CHUNK

    def spatial_gating():
        for g in range(SGU_GROUPS):
            cols = slice(g * LANES, (g + 1) * LANES)
            wm = jnp.where(t_blk >= s_blk, sw_ref[g], 0.0).astype(jnp.bfloat16)
            rhs = jnp.concatenate(
                [vn[c * SGU_CHUNK:(c + 1) * SGU_CHUNK, cols] for c in range(n_chunks)], axis=1)
            mixed = jnp.dot(wm, rhs, preferred_element_type=jnp.float32)
            for c in range(n_chunks):
                rows = slice(c * SGU_CHUNK, (c + 1) * SGU_CHUNK)
                gated = u[rows, cols] * (mixed[:, c * LANES:(c + 1) * LANES] + sb_ref[:, cols])
                ymix_ref[rows, cols] = gated.astype(jnp.bfloat16)

    for t in range(qk_w // LANES):
        cols = slice(t * LANES, (t + 1) * LANES)
        conv = cb_ref[:, cols]
        for j in range(cw_ref.shape[0]):
            conv = conv + cw_ref[j:j + 1, cols] * ext_ref[t, pl.ds(SUBLANES - 3 + j, rows_t), :]
        qk_ref[:, cols] = conv * jax.nn.sigmoid(conv)
        ext_ref[t, 0:SUBLANES, :] = ext_ref[t, rows_t:rows_t + SUBLANES, :]

    gts = z_gate.T[0:SUBLANES, :] + gb_ref[...]
    logf = pltpu.roll(jax.nn.log_sigmoid(gts), HEADS, 0)
    b_cum = _scan_lanes(logf, jnp.add, 0.0)
    g_in = gts - b_cum
    mm = jnp.maximum(mcar_ref[...], _scan_lanes(g_in, jnp.maximum, -jnp.inf))
    mm_last = jnp.concatenate(
        [jnp.broadcast_to(mm[:, (c + 1) * MIX_CHUNK - 1:(c + 1) * MIX_CHUNK], (SUBLANES, MIX_CHUNK))
         for c in range(n_chunks)], axis=1)
    mm_prev = jnp.concatenate([mcar_ref[:, :MIX_CHUNK], mm_last[:, :rows_t - MIX_CHUNK]], axis=1)
    decay = jnp.exp(mm_prev - mm_last)
    p_last = jnp.exp(g_in - mm_last)
    m_end = b_cum + mm
    mcar_ref[...] = jnp.broadcast_to(m_end[:, rows_t - 1:rows_t], (SUBLANES, rows_t))
    g_in2 = g_in * LOG2E
    mm_prev2 = mm_prev * LOG2E
    cols_t = jnp.concatenate(
        [mm * -LOG2E, m_end * -LOG2E, jnp.zeros((LANES - 2 * SUBLANES, rows_t), jnp.float32)], axis=0).T

    lane = lax.broadcasted_iota(jnp.int32, (1, LANES), 1)
    t_idx = lax.broadcasted_iota(jnp.int32, (MIX_CHUNK, MIX_CHUNK), 0)
    s_idx = lax.broadcasted_iota(jnp.int32, (MIX_CHUNK, MIX_CHUNK), 1)
    causal = s_idx <= t_idx
    ones_aug = jnp.ones((MIX_CHUNK, V_DIM), jnp.bfloat16)
    scores, local, vaugs, qs = {}, {}, {}, {}
    for c in range(n_chunks):
        rows = slice(c * MIX_CHUNK, (c + 1) * MIX_CHUNK)
        for hd in range(HEADS):
            pair = (hd // 2) * LANES
            in_head = (lane >= (hd % 2) * QK_DIM) & (lane < (hd % 2 + 1) * QK_DIM)
            k_scale = jnp.where(in_head, QK_DIM ** -0.5, 0.0)
            q = qk_ref[rows, pair:pair + LANES]
            k_t = (qk_ref[rows, HEADS * QK_DIM + pair:HEADS * QK_DIM + pair + LANES] * k_scale).T
            vcols = slice(hd * V_DIM, (hd + 1) * V_DIM)
            v_aug = jnp.concatenate([v_ref[rows, vcols], ones_aug], axis=1)
            scores[c, hd] = jnp.dot(q.astype(jnp.bfloat16), k_t.astype(jnp.bfloat16),
                                    preferred_element_type=jnp.float32)
            k_w = (k_t * p_last[hd:hd + 1, rows]).astype(jnp.bfloat16)
            local[c, hd] = jnp.dot(k_w, v_aug, preferred_element_type=jnp.float32)
            vaugs[c, hd] = v_aug
            qs[c, hd] = q
    spatial_gating()
    for c in range(n_chunks):
        rows = slice(c * MIX_CHUNK, (c + 1) * MIX_CHUNK)
        for hd in range(HEADS):
            vcols = slice(hd * V_DIM, (hd + 1) * V_DIM)
            neg_mm = jnp.broadcast_to(cols_t[rows, hd:hd + 1], (MIX_CHUNK, MIX_CHUNK))
            neg_m = jnp.broadcast_to(cols_t[rows, SUBLANES + hd:SUBLANES + hd + 1], (MIX_CHUNK, V_DIM))
            p = jnp.exp2(jnp.where(causal, g_in2[hd:hd + 1, rows] + neg_mm, -jnp.inf))
            w = (p * scores[c, hd]).astype(jnp.bfloat16)
            a_inter = jnp.exp2(mm_prev2[hd:hd + 1, rows] + neg_mm)
            st = state_ref[hd]
            lhs = jnp.concatenate([w, (qs[c, hd] * a_inter).astype(jnp.bfloat16)], axis=1)
            rhs = jnp.concatenate([vaugs[c, hd], st.astype(jnp.bfloat16)], axis=0)
            nd = jnp.dot(lhs, rhs, preferred_element_type=jnp.float32)
            num = nd[:, :V_DIM]
            den = nd[:, V_DIM:]
            hh = num / jnp.maximum(jnp.abs(den), jnp.exp2(neg_m))
            hh = hh * lax.rsqrt(jnp.mean(hh * hh, axis=-1, keepdims=True) + EPS)
            hh = hh * mh_ref[:, vcols] * og_ref[rows, vcols]
            ymix_ref[rows, sgu_w + hd * V_DIM:sgu_w + (hd + 1) * V_DIM] = hh.astype(jnp.bfloat16)
            dec = decay[hd:hd + 1, rows]
            state_ref[hd] = jnp.concatenate([dec, dec], axis=1) * st + local[c, hd]
        if (c + 1) % OUT_CHUNKS == 0:
            done = slice((c + 1 - OUT_CHUNKS) * MIX_CHUNK, (c + 1) * MIX_CHUNK)
            o_ref[done, :] = x_ref[done, :] + jnp.dot(
                ymix_ref[done, :], wout_bf[...], preferred_element_type=jnp.float32)


def _mixer(x, gain, w_in, w_gate, ln_g, ln_b, sgu_w, sgu_bias, conv_w, conv_b, gate_b, mh, w_out):
    bsz, seq, d = x.shape
    rows_t = MIX_ROWS
    assert seq % rows_t == 0 and rows_t % MIX_CHUNK == 0
    tok_spec = pl.BlockSpec((None, rows_t, d), lambda b, s: (b, s, 0))
    ins = (gain, w_in, w_gate, ln_g, ln_b, sgu_w, sgu_bias, conv_w, conv_b, gate_b, mh, w_out)
    w_main_shape = (d, w_in.shape[1] - 2 * HEADS)
    qk_w = 2 * HEADS * QK_DIM
    mv_w = HEADS * V_DIM
    return pl.pallas_call(
        _mixer_kernel,
        grid=(bsz, seq // rows_t),
        in_specs=[tok_spec] + [_resident(w_main_shape if a is w_in else a.shape) for a in ins],
        out_specs=tok_spec,
        out_shape=jax.ShapeDtypeStruct(x.shape, jnp.float32),
        scratch_shapes=[
            pltpu.VMEM((HEADS, LANES, 2 * V_DIM), jnp.float32),
            pltpu.VMEM((SUBLANES, rows_t), jnp.float32),
            pltpu.VMEM((qk_w // LANES, rows_t + SUBLANES, LANES), jnp.float32),
            pltpu.VMEM((rows_t, qk_w), jnp.float32),
            pltpu.VMEM((rows_t, mv_w), jnp.bfloat16),
            pltpu.VMEM((rows_t, mv_w), jnp.float32),
            pltpu.VMEM((rows_t, 2 * mv_w), jnp.bfloat16),
            pltpu.VMEM(w_main_shape, jnp.bfloat16),
            pltpu.VMEM(w_gate.shape, jnp.bfloat16),
            pltpu.VMEM(w_out.shape, jnp.bfloat16),
        ],
        compiler_params=pltpu.CompilerParams(
            dimension_semantics=("arbitrary", "arbitrary"), vmem_limit_bytes=VMEM_LIMIT_BYTES),
        name="mixer",
    )(x, *ins)


def kernel(x, ffn1_norm, ffn1_w_gate, ffn1_w_up, ffn1_w_down, mix_norm, w_in, sgu_ln_g, sgu_ln_b, sgu_w, sgu_b, conv_w, conv_b, igate_b, fgate_b, mh_norm, w_out, ffn2_norm, ffn2_w_gate, ffn2_w_up, ffn2_w_down, final_norm):
    bsz, seq, d = x.shape
    row = lambda a: a.reshape(1, -1)
    fin = row(final_norm)
    for l in range(ffn1_norm.shape[0]):
        x = _ffn(x.reshape(bsz * seq, d), row(ffn1_norm[l]), ffn1_w_gate[l],
                 ffn1_w_up[l], ffn1_w_down[l], fin, False).reshape(bsz, seq, d)
        n_main = w_in.shape[2] - 2 * HEADS
        w_gate = jnp.pad(w_in[l][:, n_main:], ((0, 0), (0, LANES - 2 * HEADS)))
        sgu_bias = jnp.repeat(sgu_b[l].T, LANES, axis=1)
        gate_b = jnp.broadcast_to(
            jnp.concatenate([igate_b[l], fgate_b[l]])[:, None], (2 * HEADS, MIX_ROWS))
        x = _mixer(x, row(mix_norm[l]), w_in[l], w_gate, row(sgu_ln_g[l]), row(sgu_ln_b[l]), sgu_w[l], sgu_bias,
                   conv_w[l], row(conv_b[l]), gate_b, row(mh_norm[l]), w_out[l])
        last = l == ffn1_norm.shape[0] - 1
        x = _ffn(x.reshape(bsz * seq, d), row(ffn2_norm[l]), ffn2_w_gate[l],
                 ffn2_w_up[l], ffn2_w_down[l], fin, last).reshape(bsz, seq, d)
    return x
```

```python
import functools

import jax
import jax.numpy as jnp
from jax import lax
from jax.experimental import pallas as pl
from jax.experimental.pallas import tpu as pltpu

EPS = 1e-6
LANES = 128
SUBLANES = 8
VMEM_LIMIT_BYTES = 60 * 1024 * 1024

STREAM_CHUNK = 64
SGU_CHUNK = 128
SGU_GROUPS = 4
HEADS = 4
QK_DIM = 64
V_DIM = 128
MIX_CHUNK = 128

FFN_ROWS = 1024
DOWN_ROWS = 512
FFN_COLS = 256
MIX_ROWS = 1024
OUT_CHUNKS = 4
LOG2E = 1.4426950408889634


def _resident(shape):
    zeros = (0,) * len(shape)
    return pl.BlockSpec(shape, lambda *_: zeros, pipeline_mode=pl.Buffered(1))


def _rms(x, gain):
    return x * lax.rsqrt(jnp.mean(x * x, axis=-1, keepdims=True) + EPS) * gain


def _ffn_kernel(x_ref, gain_ref, wg_ref, wu_ref, wd_ref, fin_ref, o_ref, act_ref, *, final_norm):
    x = x_ref[...]
    h = (x * gain_ref[...]).astype(jnp.bfloat16)
    rs = lax.rsqrt(jnp.mean(x * x, axis=-1, keepdims=True) + EPS)
    d_ff = wg_ref.shape[1]
    for c in range(d_ff // FFN_COLS):
        cols = slice(c * FFN_COLS, (c + 1) * FFN_COLS)
        g = rs * jnp.dot(h, wg_ref[:, cols].astype(jnp.bfloat16), preferred_element_type=jnp.float32)
        u = rs * jnp.dot(h, wu_ref[:, cols].astype(jnp.bfloat16), preferred_element_type=jnp.float32)
        act_ref[:, cols] = (g * jax.nn.sigmoid(g) * u).astype(jnp.bfloat16)
    for r in range(0, x_ref.shape[0], DOWN_ROWS):
        rows = slice(r, r + DOWN_ROWS)
        y = jnp.dot(act_ref[rows, :], wd_ref[...].astype(jnp.bfloat16), preferred_element_type=jnp.float32)
        out = x_ref[rows, :] + 0.5 * y
        if final_norm:
            out = _rms(out, fin_ref[...])
        o_ref[rows, :] = out


def _ffn(x2d, gain, wg, wu, wd, fin, final_norm):
    n, d = x2d.shape
    d_ff = wg.shape[1]
    assert n % FFN_ROWS == 0 and d_ff % FFN_COLS == 0
    row_spec = pl.BlockSpec((FFN_ROWS, d), lambda i: (i, 0))
    return pl.pallas_call(
        functools.partial(_ffn_kernel, final_norm=final_norm),
        grid=(n // FFN_ROWS,),
        in_specs=[row_spec, _resident((1, d)), _resident((d, d_ff)), _resident((d, d_ff)),
                  _resident((d_ff, d)), _resident((1, d))],
        out_specs=row_spec,
        out_shape=jax.ShapeDtypeStruct((n, d), jnp.float32),
        scratch_shapes=[pltpu.VMEM((FFN_ROWS, d_ff), jnp.bfloat16)],
        compiler_params=pltpu.CompilerParams(
            dimension_semantics=("arbitrary",), vmem_limit_bytes=VMEM_LIMIT_BYTES),
        name="ffn_final" if final_norm else "ffn",
    )(x2d, gain, wg, wu, wd, fin)


def _scan_lanes(x, combine, fill):
    width = x.shape[1]
    lane = lax.broadcasted_iota(jnp.int32, x.shape, 1)
    d = 1
    while d < width:
        x = combine(x, jnp.where(lane >= d, pltpu.roll(x, d, 1), fill))
        d *= 2
    return x


def _mixer_kernel(x_ref, gain_ref, win_ref, wgate_ref, lng_ref, lnb_ref, sw_ref, sb_ref, cw_ref, cb_ref,
                  gb_ref, mh_ref, wout_ref, o_ref,
                  state_ref, mcar_ref, ext_ref, qk_ref, v_ref, og_ref, ymix_ref, win_bf, wgate_bf, wout_bf):
    rows_t = x_ref.shape[0]
    n_chunks = rows_t // MIX_CHUNK
    sgu_w = SGU_GROUPS * LANES
    qk_w = 2 * HEADS * QK_DIM
    mv_w = HEADS * V_DIM
    o_uv, o_qk = 0, 2 * sgu_w
    o_v = o_qk + qk_w
    o_o = o_v + mv_w

    @pl.when((pl.program_id(0) == 0) & (pl.program_id(1) == 0))
    def _():
        win_bf[...] = win_ref[...].astype(jnp.bfloat16)
        wgate_bf[...] = wgate_ref[...].astype(jnp.bfloat16)
        wout_bf[...] = wout_ref[...].astype(jnp.bfloat16)

    @pl.when(pl.program_id(1) == 0)
    def _():
        state_ref[...] = jnp.zeros_like(state_ref)
        mcar_ref[...] = jnp.zeros_like(mcar_ref)
        ext_ref[:, 0:SUBLANES, :] = jnp.zeros((qk_w // LANES, SUBLANES, LANES), jnp.float32)

    x = x_ref[...]
    xg = x * gain_ref[...]
    h_raw = xg.astype(jnp.bfloat16)
    rs = lax.rsqrt(jnp.mean(x * x, axis=-1, keepdims=True) + EPS)
    h = (xg * rs).astype(jnp.bfloat16)

    def proj(lo, width):
        return jnp.dot(h, win_bf[:, lo:lo + width], preferred_element_type=jnp.float32)

    z_gate = rs * jnp.dot(h_raw, wgate_bf[...], preferred_element_type=jnp.float32)
    z_qk = rs * jnp.dot(h_raw, win_bf[:, o_qk:o_qk + qk_w], preferred_element_type=jnp.float32)
    for t in range(qk_w // LANES):
        ext_ref[t, SUBLANES:, :] = z_qk[:, t * LANES:(t + 1) * LANES]
    uv = jax.nn.gelu(proj(o_uv, 2 * sgu_w))
    v_ref[...] = proj(o_v, mv_w).astype(jnp.bfloat16)
    og_ref[...] = jax.nn.sigmoid(proj(o_o, mv_w)) * mh_ref[...]

    u = uv[:, :sgu_w]
    v = uv[:, sgu_w:]
    vc = v - jnp.mean(v, axis=-1, keepdims=True)
    vn = vc * lax.rsqrt(jnp.mean(vc * vc, axis=-1, keepdims=True) + EPS) * lng_ref[...] + lnb_ref[...]
    vn = vn.astype(jnp.bfloat16)
    t_blk = lax.broadcasted_iota(jnp.int32, (SGU_CHUNK, SGU_CHUNK), 0) // STREAM_CHUNK
    s_blk = lax.broadcasted_iota(jnp.int32, (SGU_CHUNK, SGU_CHUNK), 1) // STREAM_CHUNK

    def spatial_gating():
        for g in range(SGU_GROUPS):
            cols = slice(g * LANES, (g + 1) * LANES)
            wm = jnp.where(t_blk >= s_blk, sw_ref[g], 0.0).astype(jnp.bfloat16)
            rhs = jnp.concatenate(
                [vn[c * SGU_CHUNK:(c + 1) * SGU_CHUNK, cols] for c in range(n_chunks)], axis=1)
            mixed = jnp.dot(wm, rhs, preferred_element_type=jnp.float32)
            for c in range(n_chunks):
                rows = slice(c * SGU_CHUNK, (c + 1) * SGU_CHUNK)
                gated = u[rows, cols] * (mixed[:, c * LANES:(c + 1) * LANES] + sb_ref[:, cols])
                ymix_ref[rows, cols] = gated.astype(jnp.bfloat16)

    for t in range(qk_w // LANES):
        cols = slice(t * LANES, (t + 1) * LANES)
        conv = cb_ref[:, cols]
        for j in range(cw_ref.shape[0]):
            conv = conv + cw_ref[j:j + 1, cols] * ext_ref[t, pl.ds(SUBLANES - 3 + j, rows_t), :]
        act = conv * jax.nn.sigmoid(conv)
        if t * LANES >= HEADS * QK_DIM:
            act = act * (QK_DIM ** -0.5)
        qk_ref[:, cols] = act
        ext_ref[t, 0:SUBLANES, :] = ext_ref[t, rows_t:rows_t + SUBLANES, :]

    gts = z_gate.T[0:SUBLANES, :] + gb_ref[...]
    logf = pltpu.roll(jax.nn.log_sigmoid(gts), HEADS, 0)
    b_cum = _scan_lanes(logf, jnp.add, 0.0)
    g_in = gts - b_cum
    mm = jnp.maximum(mcar_ref[...], _scan_lanes(g_in, jnp.maximum, -jnp.inf))
    mm_last = jnp.concatenate(
        [jnp.broadcast_to(mm[:, (c + 1) * MIX_CHUNK - 1:(c + 1) * MIX_CHUNK], (SUBLANES, MIX_CHUNK))
         for c in range(n_chunks)], axis=1)
    mm_prev = jnp.concatenate([mcar_ref[:, :MIX_CHUNK], mm_last[:, :rows_t - MIX_CHUNK]], axis=1)
    decay = jnp.exp(mm_prev - mm_last)
    p_last = jnp.exp(g_in - mm_last)
    m_end = b_cum + mm
    mcar_ref[...] = jnp.broadcast_to(m_end[:, rows_t - 1:rows_t], (SUBLANES, rows_t))
    g_in2 = g_in * LOG2E
    mm_prev2 = mm_prev * LOG2E
    cols_t = jnp.concatenate(
        [mm * -LOG2E, m_end * -LOG2E, jnp.zeros((LANES - 2 * SUBLANES, rows_t), jnp.float32)], axis=0).T

    t_idx = lax.broadcasted_iota(jnp.int32, (MIX_CHUNK, MIX_CHUNK), 0)
    s_idx = lax.broadcasted_iota(jnp.int32, (MIX_CHUNK, MIX_CHUNK), 1)
    causal = s_idx <= t_idx
    ones_aug = jnp.ones((MIX_CHUNK, V_DIM), jnp.bfloat16)
    scores, local, vaugs, qs = {}, {}, {}, {}
    zero_half = jnp.zeros((QK_DIM, MIX_CHUNK), jnp.bfloat16)
    for c in range(n_chunks):
        rows = slice(c * MIX_CHUNK, (c + 1) * MIX_CHUNK)
        for pair in range(HEADS // 2):
            q = qk_ref[rows, pair * LANES:(pair + 1) * LANES]
            q_bf = q.astype(jnp.bfloat16)
            k_t = qk_ref[rows, HEADS * QK_DIM + pair * LANES:HEADS * QK_DIM + (pair + 1) * LANES].T
            for j in range(2):
                hd = 2 * pair + j
                k_half = k_t[j * QK_DIM:(j + 1) * QK_DIM, :]

                def own_rows(half):
                    return jnp.concatenate([half, zero_half] if j == 0 else [zero_half, half], axis=0)

                vcols = slice(hd * V_DIM, (hd + 1) * V_DIM)
                v_aug = jnp.concatenate([v_ref[rows, vcols], ones_aug], axis=1)
                scores[c, hd] = jnp.dot(q_bf, own_rows(k_half.astype(jnp.bfloat16)),
                                        preferred_element_type=jnp.float32)
                k_w = own_rows((k_half * p_last[hd:hd + 1, rows]).astype(jnp.bfloat16))
                local[c, hd] = jnp.dot(k_w, v_aug, preferred_element_type=jnp.float32)
                vaugs[c, hd] = v_aug
                qs[c, hd] = q
    spatial_gating()
    for c in range(n_chunks):
        rows = slice(c * MIX_CHUNK, (c + 1) * MIX_CHUNK)
        for hd in range(HEADS):
            vcols = slice(hd * V_DIM, (hd + 1) * V_DIM)
            neg_mm = jnp.broadcast_to(cols_t[rows, hd:hd + 1], (MIX_CHUNK, MIX_CHUNK))
            neg_m = jnp.broadcast_to(cols_t[rows, SUBLANES + hd:SUBLANES + hd + 1], (MIX_CHUNK, V_DIM))
            p = jnp.exp2(jnp.where(causal, g_in2[hd:hd + 1, rows] + neg_mm, -jnp.inf))
            w = (p * scores[c, hd]).astype(jnp.bfloat16)
            a_inter = jnp.exp2(mm_prev2[hd:hd + 1, rows] + neg_mm)
            st = state_ref[hd]
            lhs = jnp.concatenate([w, (qs[c, hd] * a_inter).astype(jnp.bfloat16)], axis=1)
            rhs = jnp.concatenate([vaugs[c, hd], st.astype(jnp.bfloat16)], axis=0)
            nd = jnp.dot(lhs, rhs, preferred_element_type=jnp.float32)
            num = nd[:, :V_DIM]
            den = nd[:, V_DIM:]
            hh = num / jnp.maximum(jnp.abs(den), jnp.exp2(neg_m))
            hh = hh * lax.rsqrt(jnp.mean(hh * hh, axis=-1, keepdims=True) + EPS)
            hh = hh * og_ref[rows, vcols]
            ymix_ref[rows, sgu_w + hd * V_DIM:sgu_w + (hd + 1) * V_DIM] = hh.astype(jnp.bfloat16)
            dec = decay[hd:hd + 1, rows]
            state_ref[hd] = jnp.concatenate([dec, dec], axis=1) * st + local[c, hd]
        if (c + 1) % OUT_CHUNKS == 0:
            done = slice((c + 1 - OUT_CHUNKS) * MIX_CHUNK, (c + 1) * MIX_CHUNK)
            o_ref[done, :] = x_ref[done, :] + jnp.dot(
                ymix_ref[done, :], wout_bf[...], preferred_element_type=jnp.float32)


def _mixer(x, gain, w_in, w_gate, ln_g, ln_b, sgu_w, sgu_bias, conv_w, conv_b, gate_b, mh, w_out):
    bsz, seq, d = x.shape
    rows_t = MIX_ROWS
    assert seq % rows_t == 0 and rows_t % MIX_CHUNK == 0
    tok_spec = pl.BlockSpec((None, rows_t, d), lambda b, s: (b, s, 0))
    ins = (gain, w_in, w_gate, ln_g, ln_b, sgu_w, sgu_bias, conv_w, conv_b, gate_b, mh, w_out)
    w_main_shape = (d, w_in.shape[1] - 2 * HEADS)
    qk_w = 2 * HEADS * QK_DIM
    mv_w = HEADS * V_DIM
    return pl.pallas_call(
        _mixer_kernel,
        grid=(bsz, seq // rows_t),
        in_specs=[tok_spec] + [_resident(w_main_shape if a is w_in else a.shape) for a in ins],
        out_specs=tok_spec,
        out_shape=jax.ShapeDtypeStruct(x.shape, jnp.float32),
        scratch_shapes=[
            pltpu.VMEM((HEADS, LANES, 2 * V_DIM), jnp.float32),
            pltpu.VMEM((SUBLANES, rows_t), jnp.float32),
            pltpu.VMEM((qk_w // LANES, rows_t + SUBLANES, LANES), jnp.float32),
            pltpu.VMEM((rows_t, qk_w), jnp.float32),
            pltpu.VMEM((rows_t, mv_w), jnp.bfloat16),
            pltpu.VMEM((rows_t, mv_w), jnp.float32),
            pltpu.VMEM((rows_t, 2 * mv_w), jnp.bfloat16),
            pltpu.VMEM(w_main_shape, jnp.bfloat16),
            pltpu.VMEM(w_gate.shape, jnp.bfloat16),
            pltpu.VMEM(w_out.shape, jnp.bfloat16),
        ],
        compiler_params=pltpu.CompilerParams(
            dimension_semantics=("arbitrary", "arbitrary"), vmem_limit_bytes=VMEM_LIMIT_BYTES),
        name="mixer",
    )(x, *ins)


def kernel(x, ffn1_norm, ffn1_w_gate, ffn1_w_up, ffn1_w_down, mix_norm, w_in, sgu_ln_g, sgu_ln_b, sgu_w, sgu_b, conv_w, conv_b, igate_b, fgate_b, mh_norm, w_out, ffn2_norm, ffn2_w_gate, ffn2_w_up, ffn2_w_down, final_norm):
    bsz, seq, d = x.shape
    row = lambda a: a.reshape(1, -1)
    fin = row(final_norm)
    for l in range(ffn1_norm.shape[0]):
        x = _ffn(x.reshape(bsz * seq, d), row(ffn1_norm[l]), ffn1_w_gate[l],
                 ffn1_w_up[l], ffn1_w_down[l], fin, False).reshape(bsz, seq, d)
        n_main = w_in.shape[2] - 2 * HEADS
        w_gate = jnp.pad(w_in[l][:, n_main:], ((0, 0), (0, LANES - 2 * HEADS)))
        sgu_bias = jnp.repeat(sgu_b[l].T, LANES, axis=1)
        gate_b = jnp.broadcast_to(
            jnp.concatenate([igate_b[l], fgate_b[l]])[:, None], (2 * HEADS, MIX_ROWS))
        x = _mixer(x, row(mix_norm[l]), w_in[l], w_gate, row(sgu_ln_g[l]), row(sgu_ln_b[l]), sgu_w[l], sgu_bias,
                   conv_w[l], row(conv_b[l]), gate_b, row(mh_norm[l]), w_out[l])
        last = l == ffn1_norm.shape[0] - 1
        x = _ffn(x.reshape(bsz * seq, d), row(ffn2_norm[l]), ffn2_w_gate[l],
                 ffn2_w_up[l], ffn2_w_down[l], fin, last).reshape(bsz, seq, d)
    return x
```

```python
import functools

import jax
import jax.numpy as jnp
from jax import lax
from jax.experimental import pallas as pl
from jax.experimental.pallas import tpu as pltpu

EPS = 1e-6
LANES = 128
SUBLANES = 8
VMEM_LIMIT_BYTES = 60 * 1024 * 1024

STREAM_CHUNK = 64
SGU_CHUNK = 128
SGU_GROUPS = 4
HEADS = 4
QK_DIM = 64
V_DIM = 128
MIX_CHUNK = 128

FFN_ROWS = 1024
DOWN_ROWS = 512
FFN_COLS = 256
MIX_ROWS = 1024
OUT_CHUNKS = 4
LOG2E = 1.4426950408889634


def _resident(shape):
    zeros = (0,) * len(shape)
    return pl.BlockSpec(shape, lambda *_: zeros, pipeline_mode=pl.Buffered(1))


def _rms(x, gain):
    return x * lax.rsqrt(jnp.mean(x * x, axis=-1, keepdims=True) + EPS) * gain


def _ffn_kernel(x_ref, gain_ref, wg_ref, wu_ref, wd_ref, fin_ref, o_ref, act_ref, *, final_norm):
    x = x_ref[...]
    h = (x * gain_ref[...]).astype(jnp.bfloat16)
    rs = lax.rsqrt(jnp.mean(x * x, axis=-1, keepdims=True) + EPS)
    d_ff = wg_ref.shape[1]
    for c in range(d_ff // FFN_COLS):
        cols = slice(c * FFN_COLS, (c + 1) * FFN_COLS)
        g = rs * jnp.dot(h, wg_ref[:, cols].astype(jnp.bfloat16), preferred_element_type=jnp.float32)
        u = rs * jnp.dot(h, wu_ref[:, cols].astype(jnp.bfloat16), preferred_element_type=jnp.float32)
        act_ref[:, cols] = (g * jax.nn.sigmoid(g) * u).astype(jnp.bfloat16)
    for r in range(0, x_ref.shape[0], DOWN_ROWS):
        rows = slice(r, r + DOWN_ROWS)
        y = jnp.dot(act_ref[rows, :], wd_ref[...].astype(jnp.bfloat16), preferred_element_type=jnp.float32)
        out = x_ref[rows, :] + 0.5 * y
        if final_norm:
            out = _rms(out, fin_ref[...])
        o_ref[rows, :] = out


def _ffn(x2d, gain, wg, wu, wd, fin, final_norm):
    n, d = x2d.shape
    d_ff = wg.shape[1]
    assert n % FFN_ROWS == 0 and d_ff % FFN_COLS == 0
    row_spec = pl.BlockSpec((FFN_ROWS, d), lambda i: (i, 0))
    return pl.pallas_call(
        functools.partial(_ffn_kernel, final_norm=final_norm),
        grid=(n // FFN_ROWS,),
        in_specs=[row_spec, _resident((1, d)), _resident((d, d_ff)), _resident((d, d_ff)),
                  _resident((d_ff, d)), _resident((1, d))],
        out_specs=row_spec,
        out_shape=jax.ShapeDtypeStruct((n, d), jnp.float32),
        scratch_shapes=[pltpu.VMEM((FFN_ROWS, d_ff), jnp.bfloat16)],
        compiler_params=pltpu.CompilerParams(
            dimension_semantics=("arbitrary",), vmem_limit_bytes=VMEM_LIMIT_BYTES),
        name="ffn_final" if final_norm else "ffn",
    )(x2d, gain, wg, wu, wd, fin)


def _scan_lanes(x, combine, fill):
    width = x.shape[1]
    lane = lax.broadcasted_iota(jnp.int32, x.shape, 1)
    d = 1
    while d < width:
        x = combine(x, jnp.where(lane >= d, pltpu.roll(x, d, 1), fill))
        d *= 2
    return x


def _mixer_kernel(x_ref, gain_ref, win_ref, wgate_ref, lng_ref, lnb_ref, sw_ref, sb_ref, cw_ref, cb_ref,
                  gb_ref, mh_ref, wout_ref, o_ref,
                  state_ref, mcar_ref, ext_ref, qk_ref, v_ref, og_ref, ymix_ref, win_bf, wgate_bf, wout_bf):
    rows_t = x_ref.shape[0]
    n_chunks = rows_t // MIX_CHUNK
    sgu_w = SGU_GROUPS * LANES
    qk_w = 2 * HEADS * QK_DIM
    mv_w = HEADS * V_DIM
    o_uv, o_qk = 0, 2 * sgu_w
    o_v = o_qk + qk_w
    o_o = o_v + mv_w

    @pl.when((pl.program_id(0) == 0) & (pl.program_id(1) == 0))
    def _():
        win_bf[...] = win_ref[...].astype(jnp.bfloat16)
        wgate_bf[...] = wgate_ref[...].astype(jnp.bfloat16)
        wout_bf[...] = wout_ref[...].astype(jnp.bfloat16)

    @pl.when(pl.program_id(1) == 0)
    def _():
        state_ref[...] = jnp.zeros_like(state_ref)
        mcar_ref[...] = jnp.zeros_like(mcar_ref)
        ext_ref[:, 0:SUBLANES, :] = jnp.zeros((qk_w // LANES, SUBLANES, LANES), jnp.float32)

    x = x_ref[...]
    xg = x * gain_ref[...]
    h_raw = xg.astype(jnp.bfloat16)
    rs = lax.rsqrt(jnp.mean(x * x, axis=-1, keepdims=True) + EPS)
    h = (xg * rs).astype(jnp.bfloat16)

    def proj(lo, width):
        return jnp.dot(h, win_bf[:, lo:lo + width], preferred_element_type=jnp.float32)

    z_gate = rs * jnp.dot(h_raw, wgate_bf[...], preferred_element_type=jnp.float32)
    z_qk = rs * jnp.dot(h_raw, win_bf[:, o_qk:o_qk + qk_w], preferred_element_type=jnp.float32)
    for t in range(qk_w // LANES):
        ext_ref[t, SUBLANES:, :] = z_qk[:, t * LANES:(t + 1) * LANES]
    uv = jax.nn.gelu(proj(o_uv, 2 * sgu_w))
    v_ref[...] = proj(o_v, mv_w).astype(jnp.bfloat16)
    og_ref[...] = jax.nn.sigmoid(proj(o_o, mv_w)) * mh_ref[...]

    u = uv[:, :sgu_w]
    v = uv[:, sgu_w:]
    vc = v - jnp.mean(v, axis=-1, keepdims=True)
    vn = vc * lax.rsqrt(jnp.mean(vc * vc, axis=-1, keepdims=True) + EPS) * lng_ref[...] + lnb_ref[...]
    vn = vn.astype(jnp.bfloat16)
    t_blk = lax.broadcasted_iota(jnp.int32, (SGU_CHUNK, SGU_CHUNK), 0) // STREAM_CHUNK
    s_blk = lax.broadcasted_iota(jnp.int32, (SGU_CHUNK, SGU_CHUNK), 1) // STREAM_CHUNK

    def spatial_gating():
        for g in range(SGU_GROUPS):
            cols = slice(g * LANES, (g + 1) * LANES)
            wm = jnp.where(t_blk >= s_blk, sw_ref[g], 0.0).astype(jnp.bfloat16)
            rhs = jnp.concatenate(
                [vn[c * SGU_CHUNK:(c + 1) * SGU_CHUNK, cols] for c in range(n_chunks)], axis=1)
            mixed = jnp.dot(wm, rhs, preferred_element_type=jnp.float32)
            for c in range(n_chunks):
                rows = slice(c * SGU_CHUNK, (c + 1) * SGU_CHUNK)
                gated = u[rows, cols] * (mixed[:, c * LANES:(c + 1) * LANES] + sb_ref[:, cols])
                ymix_ref[rows, cols] = gated.astype(jnp.bfloat16)

    for t in range(qk_w // LANES):
        cols = slice(t * LANES, (t + 1) * LANES)
        conv = cb_ref[:, cols]
        for j in range(cw_ref.shape[0]):
            conv = conv + cw_ref[j:j + 1, cols] * ext_ref[t, pl.ds(SUBLANES - 3 + j, rows_t), :]
        act = conv * jax.nn.sigmoid(conv)
        if t * LANES >= HEADS * QK_DIM:
            act = act * (QK_DIM ** -0.5)
        qk_ref[:, cols] = act
        ext_ref[t, 0:SUBLANES, :] = ext_ref[t, rows_t:rows_t + SUBLANES, :]

    gts = z_gate.T[0:SUBLANES, :] + gb_ref[...]
    logf = pltpu.roll(jax.nn.log_sigmoid(gts), HEADS, 0)
    b_cum = _scan_lanes(logf, jnp.add, 0.0)
    g_in = gts - b_cum
    mm = jnp.maximum(mcar_ref[...], _scan_lanes(g_in, jnp.maximum, -jnp.inf))
    mm_last = jnp.concatenate(
        [jnp.broadcast_to(mm[:, (c + 1) * MIX_CHUNK - 1:(c + 1) * MIX_CHUNK], (SUBLANES, MIX_CHUNK))
         for c in range(n_chunks)], axis=1)
    mm_prev = jnp.concatenate([mcar_ref[:, :MIX_CHUNK], mm_last[:, :rows_t - MIX_CHUNK]], axis=1)
    decay = jnp.exp(mm_prev - mm_last)
    p_last = jnp.exp(g_in - mm_last)
    m_end = b_cum + mm
    mcar_ref[...] = jnp.broadcast_to(m_end[:, rows_t - 1:rows_t], (SUBLANES, rows_t))
    g_in2 = g_in * LOG2E
    mm_prev2 = mm_prev * LOG2E
    cols_t = jnp.concatenate(
        [mm * -LOG2E, m_end * -LOG2E, jnp.zeros((LANES - 2 * SUBLANES, rows_t), jnp.float32)], axis=0).T

    t_idx = lax.broadcasted_iota(jnp.int32, (MIX_CHUNK, MIX_CHUNK), 0)
    s_idx = lax.broadcasted_iota(jnp.int32, (MIX_CHUNK, MIX_CHUNK), 1)
    causal = s_idx <= t_idx
    ones_aug = jnp.ones((MIX_CHUNK, V_DIM), jnp.bfloat16)
    scores, local, vaugs, qs = {}, {}, {}, {}
    zero_half = jnp.zeros((QK_DIM, MIX_CHUNK), jnp.bfloat16)
    zero_state = jnp.zeros((QK_DIM, 2 * V_DIM), jnp.bfloat16)
    for c in range(n_chunks):
        rows = slice(c * MIX_CHUNK, (c + 1) * MIX_CHUNK)
        for pair in range(HEADS // 2):
            q = qk_ref[rows, pair * LANES:(pair + 1) * LANES]
            q_bf = q.astype(jnp.bfloat16)
            k_t = qk_ref[rows, HEADS * QK_DIM + pair * LANES:HEADS * QK_DIM + (pair + 1) * LANES].T
            for j in range(2):
                hd = 2 * pair + j
                k_half = k_t[j * QK_DIM:(j + 1) * QK_DIM, :]

                def own_rows(half):
                    return jnp.concatenate([half, zero_half] if j == 0 else [zero_half, half], axis=0)

                vcols = slice(hd * V_DIM, (hd + 1) * V_DIM)
                v_aug = jnp.concatenate([v_ref[rows, vcols], ones_aug], axis=1)
                scores[c, hd] = jnp.dot(q_bf, own_rows(k_half.astype(jnp.bfloat16)),
                                        preferred_element_type=jnp.float32)
                k_w = (k_half * p_last[hd:hd + 1, rows]).astype(jnp.bfloat16)
                local[c, hd] = jnp.dot(k_w, v_aug, preferred_element_type=jnp.float32)
                vaugs[c, hd] = v_aug
                qs[c, hd] = q
    spatial_gating()
    for c in range(n_chunks):
        rows = slice(c * MIX_CHUNK, (c + 1) * MIX_CHUNK)
        for hd in range(HEADS):
            vcols = slice(hd * V_DIM, (hd + 1) * V_DIM)
            neg_mm = jnp.broadcast_to(cols_t[rows, hd:hd + 1], (MIX_CHUNK, MIX_CHUNK))
            neg_m = jnp.broadcast_to(cols_t[rows, SUBLANES + hd:SUBLANES + hd + 1], (MIX_CHUNK, V_DIM))
            p = jnp.exp2(jnp.where(causal, g_in2[hd:hd + 1, rows] + neg_mm, -jnp.inf))
            w = (p * scores[c, hd]).astype(jnp.bfloat16)
            a_inter = jnp.exp2(mm_prev2[hd:hd + 1, rows] + neg_mm)
            st = state_ref[hd]
            lhs = jnp.concatenate([w, (qs[c, hd] * a_inter).astype(jnp.bfloat16)], axis=1)
            st_bf = st.astype(jnp.bfloat16)
            st_rows = [st_bf, zero_state] if hd % 2 == 0 else [zero_state, st_bf]
            rhs = jnp.concatenate([vaugs[c, hd]] + st_rows, axis=0)
            nd = jnp.dot(lhs, rhs, preferred_element_type=jnp.float32)
            num = nd[:, :V_DIM]
            den = nd[:, V_DIM:]
            hh = num / jnp.maximum(jnp.abs(den), jnp.exp2(neg_m))
            hh = hh * lax.rsqrt(jnp.mean(hh * hh, axis=-1, keepdims=True) + EPS)
            hh = hh * og_ref[rows, vcols]
            ymix_ref[rows, sgu_w + hd * V_DIM:sgu_w + (hd + 1) * V_DIM] = hh.astype(jnp.bfloat16)
            dec = decay[hd:hd + 1, rows]
            state_ref[hd] = jnp.concatenate([dec, dec], axis=1) * st + local[c, hd]
        if (c + 1) % OUT_CHUNKS == 0:
            done = slice((c + 1 - OUT_CHUNKS) * MIX_CHUNK, (c + 1) * MIX_CHUNK)
            o_ref[done, :] = x_ref[done, :] + jnp.dot(
                ymix_ref[done, :], wout_bf[...], preferred_element_type=jnp.float32)


def _mixer(x, gain, w_in, w_gate, ln_g, ln_b, sgu_w, sgu_bias, conv_w, conv_b, gate_b, mh, w_out):
    bsz, seq, d = x.shape
    rows_t = MIX_ROWS
    assert seq % rows_t == 0 and rows_t % MIX_CHUNK == 0
    tok_spec = pl.BlockSpec((None, rows_t, d), lambda b, s: (b, s, 0))
    ins = (gain, w_in, w_gate, ln_g, ln_b, sgu_w, sgu_bias, conv_w, conv_b, gate_b, mh, w_out)
    w_main_shape = (d, w_in.shape[1] - 2 * HEADS)
    qk_w = 2 * HEADS * QK_DIM
    mv_w = HEADS * V_DIM
    return pl.pallas_call(
        _mixer_kernel,
        grid=(bsz, seq // rows_t),
        in_specs=[tok_spec] + [_resident(w_main_shape if a is w_in else a.shape) for a in ins],
        out_specs=tok_spec,
        out_shape=jax.ShapeDtypeStruct(x.shape, jnp.float32),
        scratch_shapes=[
            pltpu.VMEM((HEADS, QK_DIM, 2 * V_DIM), jnp.float32),
            pltpu.VMEM((SUBLANES, rows_t), jnp.float32),
            pltpu.VMEM((qk_w // LANES, rows_t + SUBLANES, LANES), jnp.float32),
            pltpu.VMEM((rows_t, qk_w), jnp.float32),
            pltpu.VMEM((rows_t, mv_w), jnp.bfloat16),
            pltpu.VMEM((rows_t, mv_w), jnp.float32),
            pltpu.VMEM((rows_t, 2 * mv_w), jnp.bfloat16),
            pltpu.VMEM(w_main_shape, jnp.bfloat16),
            pltpu.VMEM(w_gate.shape, jnp.bfloat16),
            pltpu.VMEM(w_out.shape, jnp.bfloat16),
        ],
        compiler_params=pltpu.CompilerParams(
            dimension_semantics=("arbitrary", "arbitrary"), vmem_limit_bytes=VMEM_LIMIT_BYTES),
        name="mixer",
    )(x, *ins)


def kernel(x, ffn1_norm, ffn1_w_gate, ffn1_w_up, ffn1_w_down, mix_norm, w_in, sgu_ln_g, sgu_ln_b, sgu_w, sgu_b, conv_w, conv_b, igate_b, fgate_b, mh_norm, w_out, ffn2_norm, ffn2_w_gate, ffn2_w_up, ffn2_w_down, final_norm):
    bsz, seq, d = x.shape
    row = lambda a: a.reshape(1, -1)
    fin = row(final_norm)
    for l in range(ffn1_norm.shape[0]):
        x = _ffn(x.reshape(bsz * seq, d), row(ffn1_norm[l]), ffn1_w_gate[l],
                 ffn1_w_up[l], ffn1_w_down[l], fin, False).reshape(bsz, seq, d)
        n_main = w_in.shape[2] - 2 * HEADS
        w_gate = jnp.pad(w_in[l][:, n_main:], ((0, 0), (0, LANES - 2 * HEADS)))
        sgu_bias = jnp.repeat(sgu_b[l].T, LANES, axis=1)
        gate_b = jnp.broadcast_to(
            jnp.concatenate([igate_b[l], fgate_b[l]])[:, None], (2 * HEADS, MIX_ROWS))
        x = _mixer(x, row(mix_norm[l]), w_in[l], w_gate, row(sgu_ln_g[l]), row(sgu_ln_b[l]), sgu_w[l], sgu_bias,
                   conv_w[l], row(conv_b[l]), gate_b, row(mh_norm[l]), w_out[l])
        last = l == ffn1_norm.shape[0] - 1
        x = _ffn(x.reshape(bsz * seq, d), row(ffn2_norm[l]), ffn2_w_gate[l],
                 ffn2_w_up[l], ffn2_w_down[l], fin, last).reshape(bsz, seq, d)
    return x
```

```python
import functools

import jax
import jax.numpy as jnp
from jax import lax
from jax.experimental import pallas as pl
from jax.experimental.pallas import tpu as pltpu

EPS = 1e-6
LANES = 128
SUBLANES = 8
VMEM_LIMIT_BYTES = 60 * 1024 * 1024

STREAM_CHUNK = 64
SGU_CHUNK = 128
SGU_GROUPS = 4
HEADS = 4
QK_DIM = 64
V_DIM = 128
MIX_CHUNK = 128

FFN_ROWS = 1024
DOWN_ROWS = 512
FFN_COLS = 256
MIX_ROWS = 1024
OUT_CHUNKS = 4
LOG2E = 1.4426950408889634


def _resident(shape):
    zeros = (0,) * len(shape)
    return pl.BlockSpec(shape, lambda *_: zeros, pipeline_mode=pl.Buffered(1))


def _rms(x, gain):
    return x * lax.rsqrt(jnp.mean(x * x, axis=-1, keepdims=True) + EPS) * gain


def _ffn_kernel(x_ref, gain_ref, wg_hbm, wu_hbm, wd_hbm, fin_ref, o_ref,
                act_ref, wg_ref, wu_ref, wd_ref, sem, *, final_norm):
    d_ff = wg_ref.shape[1]
    n_chunks = d_ff // FFN_COLS

    def chunk_copies(c):
        cols = pl.ds(c * FFN_COLS, FFN_COLS)
        return (pltpu.make_async_copy(wg_hbm.at[:, cols], wg_ref.at[:, cols], sem.at[c]),
                pltpu.make_async_copy(wu_hbm.at[:, cols], wu_ref.at[:, cols], sem.at[n_chunks + c]))

    def down_copy():
        return pltpu.make_async_copy(wd_hbm, wd_ref, sem.at[2 * n_chunks])

    def body(first_step):
        if first_step:
            for c in range(n_chunks):
                for copy in chunk_copies(c):
                    copy.start()
            down_copy().start()
        x = x_ref[...]
        h = (x * gain_ref[...]).astype(jnp.bfloat16)
        rs = lax.rsqrt(jnp.mean(x * x, axis=-1, keepdims=True) + EPS)
        for c in range(n_chunks):
            cols = slice(c * FFN_COLS, (c + 1) * FFN_COLS)
            if first_step:
                for copy in chunk_copies(c):
                    copy.wait()
            g = rs * jnp.dot(h, wg_ref[:, cols].astype(jnp.bfloat16), preferred_element_type=jnp.float32)
            u = rs * jnp.dot(h, wu_ref[:, cols].astype(jnp.bfloat16), preferred_element_type=jnp.float32)
            act_ref[:, cols] = (g * jax.nn.sigmoid(g) * u).astype(jnp.bfloat16)
        if first_step:
            down_copy().wait()
        for r in range(0, x_ref.shape[0], DOWN_ROWS):
            rows = slice(r, r + DOWN_ROWS)
            y = jnp.dot(act_ref[rows, :], wd_ref[...].astype(jnp.bfloat16), preferred_element_type=jnp.float32)
            out = x_ref[rows, :] + 0.5 * y
            if final_norm:
                out = _rms(out, fin_ref[...])
            o_ref[rows, :] = out

    pl.when(pl.program_id(0) == 0)(lambda: body(True))
    pl.when(pl.program_id(0) > 0)(lambda: body(False))


def _ffn(x2d, gain, wg, wu, wd, fin, final_norm):
    n, d = x2d.shape
    d_ff = wg.shape[1]
    assert n % FFN_ROWS == 0 and d_ff % FFN_COLS == 0
    row_spec = pl.BlockSpec((FFN_ROWS, d), lambda i: (i, 0))
    in_hbm = pl.BlockSpec(memory_space=pl.ANY)
    return pl.pallas_call(
        functools.partial(_ffn_kernel, final_norm=final_norm),
        grid=(n // FFN_ROWS,),
        in_specs=[row_spec, _resident((1, d)), in_hbm, in_hbm, in_hbm, _resident((1, d))],
        out_specs=row_spec,
        out_shape=jax.ShapeDtypeStruct((n, d), jnp.float32),
        scratch_shapes=[pltpu.VMEM((FFN_ROWS, d_ff), jnp.bfloat16),
                        pltpu.VMEM((d, d_ff), jnp.float32), pltpu.VMEM((d, d_ff), jnp.float32),
                        pltpu.VMEM((d_ff, d), jnp.float32),
                        pltpu.SemaphoreType.DMA((2 * (d_ff // FFN_COLS) + 1,))],
        compiler_params=pltpu.CompilerParams(
            dimension_semantics=("arbitrary",), vmem_limit_bytes=VMEM_LIMIT_BYTES),
        name="ffn_final" if final_norm else "ffn",
    )(x2d, gain, wg, wu, wd, fin)


def _scan_lanes(x, combine, fill):
    width = x.shape[1]
    lane = lax.broadcasted_iota(jnp.int32, x.shape, 1)
    d = 1
    while d < width:
        x = combine(x, jnp.where(lane >= d, pltpu.roll(x, d, 1), fill))
        d *= 2
    return x


def _mixer_kernel(x_ref, gain_ref, win_ref, wgate_ref, lng_ref, lnb_ref, sw_ref, sb_ref, cw_ref, cb_ref,
                  gb_ref, mh_ref, wout_ref, o_ref,
                  state_ref, mcar_ref, ext_ref, qk_ref, v_ref, og_ref, ymix_ref, win_bf, wgate_bf, wout_bf):
    rows_t = x_ref.shape[0]
    n_chunks = rows_t // MIX_CHUNK
    sgu_w = SGU_GROUPS * LANES
    qk_w = 2 * HEADS * QK_DIM
    mv_w = HEADS * V_DIM
    o_uv, o_qk = 0, 2 * sgu_w
    o_v = o_qk + qk_w
    o_o = o_v + mv_w

    @pl.when((pl.program_id(0) == 0) & (pl.program_id(1) == 0))
    def _():
        win_bf[...] = win_ref[...].T.astype(jnp.bfloat16)
        gate_rows = jnp.concatenate(
            [wgate_ref[...], jnp.zeros((LANES - wgate_ref.shape[0], wgate_ref.shape[1]), jnp.float32)], axis=0)
        wgate_bf[...] = gate_rows.T.astype(jnp.bfloat16)
        wout_bf[...] = wout_ref[...].astype(jnp.bfloat16)

    @pl.when(pl.program_id(1) == 0)
    def _():
        state_ref[...] = jnp.zeros_like(state_ref)
        mcar_ref[...] = jnp.zeros_like(mcar_ref)
        ext_ref[:, 0:SUBLANES, :] = jnp.zeros((qk_w // LANES, SUBLANES, LANES), jnp.float32)

    x = x_ref[...]
    xg = x * gain_ref[...]
    h_raw = xg.astype(jnp.bfloat16)
    rs = lax.rsqrt(jnp.mean(x * x, axis=-1, keepdims=True) + EPS)
    h = (xg * rs).astype(jnp.bfloat16)

    def proj(lo, width):
        return jnp.dot(h, win_bf[:, lo:lo + width], preferred_element_type=jnp.float32)

    z_gate = rs * jnp.dot(h_raw, wgate_bf[...], preferred_element_type=jnp.float32)
    z_qk = rs * jnp.dot(h_raw, win_bf[:, o_qk:o_qk + qk_w], preferred_element_type=jnp.float32)
    for t in range(qk_w // LANES):
        ext_ref[t, SUBLANES:, :] = z_qk[:, t * LANES:(t + 1) * LANES]
    uv = jax.nn.gelu(proj(o_uv, 2 * sgu_w))
    v_ref[...] = proj(o_v, mv_w).astype(jnp.bfloat16)
    og_ref[...] = jax.nn.sigmoid(proj(o_o, mv_w)) * mh_ref[...]

    u = uv[:, :sgu_w]
    v = uv[:, sgu_w:]
    vc = v - jnp.mean(v, axis=-1, keepdims=True)
    vn = vc * lax.rsqrt(jnp.mean(vc * vc, axis=-1, keepdims=True) + EPS) * lng_ref[...] + lnb_ref[...]
    vn = vn.astype(jnp.bfloat16)
    t_blk = lax.broadcasted_iota(jnp.int32, (SGU_CHUNK, SGU_CHUNK), 0) // STREAM_CHUNK
    s_blk = lax.broadcasted_iota(jnp.int32, (SGU_CHUNK, SGU_CHUNK), 1) // STREAM_CHUNK

    def spatial_gating():
        for g in range(SGU_GROUPS):
            cols = slice(g * LANES, (g + 1) * LANES)
            wm = jnp.where(t_blk >= s_blk, sw_ref[g], 0.0).astype(jnp.bfloat16)
            rhs = jnp.concatenate(
                [vn[c * SGU_CHUNK:(c + 1) * SGU_CHUNK, cols] for c in range(n_chunks)], axis=1)
            mixed = jnp.dot(wm, rhs, preferred_element_type=jnp.float32)
            for c in range(n_chunks):
                rows = slice(c * SGU_CHUNK, (c + 1) * SGU_CHUNK)
                gated = u[rows, cols] * (mixed[:, c * LANES:(c + 1) * LANES] + sb_ref[:, cols])
                ymix_ref[rows, cols] = gated.astype(jnp.bfloat16)

    for t in range(qk_w // LANES):
        cols = slice(t * LANES, (t + 1) * LANES)
        conv = cb_ref[:, cols]
        for j in range(cw_ref.shape[0]):
            conv = conv + cw_ref[j:j + 1, cols] * ext_ref[t, pl.ds(SUBLANES - 3 + j, rows_t), :]
        act = conv * jax.nn.sigmoid(conv)
        if t * LANES >= HEADS * QK_DIM:
            act = act * (QK_DIM ** -0.5)
        qk_ref[:, cols] = act
        ext_ref[t, 0:SUBLANES, :] = ext_ref[t, rows_t:rows_t + SUBLANES, :]

    gts = z_gate.T[0:SUBLANES, :] + gb_ref[...]
    logf = pltpu.roll(jax.nn.log_sigmoid(gts), HEADS, 0)
    b_cum = _scan_lanes(logf, jnp.add, 0.0)
    g_in = gts - b_cum
    mm = jnp.maximum(mcar_ref[...], _scan_lanes(g_in, jnp.maximum, -jnp.inf))
    mm_last = jnp.concatenate(
        [jnp.broadcast_to(mm[:, (c + 1) * MIX_CHUNK - 1:(c + 1) * MIX_CHUNK], (SUBLANES, MIX_CHUNK))
         for c in range(n_chunks)], axis=1)
    mm_prev = jnp.concatenate([mcar_ref[:, :MIX_CHUNK], mm_last[:, :rows_t - MIX_CHUNK]], axis=1)
    decay = jnp.exp(mm_prev - mm_last)
    p_last = jnp.exp(g_in - mm_last)
    m_end = b_cum + mm
    mcar_ref[...] = jnp.broadcast_to(m_end[:, rows_t - 1:rows_t], (SUBLANES, rows_t))
    g_in2 = g_in * LOG2E
    mm_prev2 = mm_prev * LOG2E
    cols_t = jnp.concatenate(
        [mm * -LOG2E, m_end * -LOG2E, jnp.zeros((LANES - 2 * SUBLANES, rows_t), jnp.float32)], axis=0).T

    t_idx = lax.broadcasted_iota(jnp.int32, (MIX_CHUNK, MIX_CHUNK), 0)
    s_idx = lax.broadcasted_iota(jnp.int32, (MIX_CHUNK, MIX_CHUNK), 1)
    causal = s_idx <= t_idx
    ones_aug = jnp.ones((MIX_CHUNK, V_DIM), jnp.bfloat16)
    scores, local = {}, {}
    zero_half = jnp.zeros((QK_DIM, MIX_CHUNK), jnp.bfloat16)
    zero_state = jnp.zeros((QK_DIM, 2 * V_DIM), jnp.bfloat16)
    for c in range(n_chunks):
        rows = slice(c * MIX_CHUNK, (c + 1) * MIX_CHUNK)
        for pair in range(HEADS // 2):
            q = qk_ref[rows, pair * LANES:(pair + 1) * LANES]
            q_bf = q.astype(jnp.bfloat16)
            k_t = qk_ref[rows, HEADS * QK_DIM + pair * LANES:HEADS * QK_DIM + (pair + 1) * LANES].T
            k_bf = k_t.astype(jnp.bfloat16)
            k_both = jnp.concatenate(
                [jnp.concatenate([k_bf[:QK_DIM, :], zero_half], axis=0),
                 jnp.concatenate([zero_half, k_bf[QK_DIM:, :]], axis=0)], axis=1)
            s_both = jnp.dot(q_bf, k_both, preferred_element_type=jnp.float32)
            for j in range(2):
                hd = 2 * pair + j
                k_half = k_t[j * QK_DIM:(j + 1) * QK_DIM, :]
                vcols = slice(hd * V_DIM, (hd + 1) * V_DIM)
                v_aug = jnp.concatenate([v_ref[rows, vcols], ones_aug], axis=1)
                scores[c, hd] = s_both[:, j * MIX_CHUNK:(j + 1) * MIX_CHUNK]
                k_w = (k_half * p_last[hd:hd + 1, rows]).astype(jnp.bfloat16)
                local[c, hd] = jnp.dot(k_w, v_aug, preferred_element_type=jnp.float32)
    spatial_gating()
    for c in range(n_chunks):
        rows = slice(c * MIX_CHUNK, (c + 1) * MIX_CHUNK)
        for hd in range(HEADS):
            vcols = slice(hd * V_DIM, (hd + 1) * V_DIM)
            neg_mm = jnp.broadcast_to(cols_t[rows, hd:hd + 1], (MIX_CHUNK, MIX_CHUNK))
            neg_m = jnp.broadcast_to(cols_t[rows, SUBLANES + hd:SUBLANES + hd + 1], (MIX_CHUNK, V_DIM))
            p = jnp.exp2(jnp.where(causal, g_in2[hd:hd + 1, rows] + neg_mm, -jnp.inf))
            w = (p * scores[c, hd]).astype(jnp.bfloat16)
            a_inter = jnp.exp2(mm_prev2[hd:hd + 1, rows] + neg_mm)
            st = state_ref[hd]
            q = qk_ref[rows, (hd // 2) * LANES:(hd // 2 + 1) * LANES]
            lhs = jnp.concatenate([w, (q * a_inter).astype(jnp.bfloat16)], axis=1)
            st_bf = st.astype(jnp.bfloat16)
            st_rows = [st_bf, zero_state] if hd % 2 == 0 else [zero_state, st_bf]
            v_aug = jnp.concatenate([v_ref[rows, vcols], ones_aug], axis=1)
            rhs = jnp.concatenate([v_aug] + st_rows, axis=0)
            nd = jnp.dot(lhs, rhs, preferred_element_type=jnp.float32)
            num = nd[:, :V_DIM]
            den = nd[:, V_DIM:]
            hh = num / jnp.maximum(jnp.abs(den), jnp.exp2(neg_m))
            hh = hh * lax.rsqrt(jnp.mean(hh * hh, axis=-1, keepdims=True) + EPS)
            hh = hh * og_ref[rows, vcols]
            ymix_ref[rows, sgu_w + hd * V_DIM:sgu_w + (hd + 1) * V_DIM] = hh.astype(jnp.bfloat16)
            dec = decay[hd:hd + 1, rows]
            state_ref[hd] = jnp.concatenate([dec, dec], axis=1) * st + local[c, hd]
        if (c + 1) % OUT_CHUNKS == 0:
            done = slice((c + 1 - OUT_CHUNKS) * MIX_CHUNK, (c + 1) * MIX_CHUNK)
            o_ref[done, :] = x_ref[done, :] + jnp.dot(
                ymix_ref[done, :], wout_bf[...], preferred_element_type=jnp.float32)


def _mixer(x, gain, w_in, w_gate, ln_g, ln_b, sgu_w, sgu_bias, conv_w, conv_b, gate_b, mh, w_out):
    bsz, seq, d = x.shape
    rows_t = MIX_ROWS
    assert seq % rows_t == 0 and rows_t % MIX_CHUNK == 0
    tok_spec = pl.BlockSpec((None, rows_t, d), lambda b, s: (b, s, 0))
    ins = (gain, w_in, w_gate, ln_g, ln_b, sgu_w, sgu_bias, conv_w, conv_b, gate_b, mh, w_out)
    n_main = w_in.shape[0] - 2 * HEADS
    qk_w = 2 * HEADS * QK_DIM
    mv_w = HEADS * V_DIM
    return pl.pallas_call(
        _mixer_kernel,
        grid=(bsz, seq // rows_t),
        in_specs=[tok_spec, _resident(gain.shape), _resident((n_main, d)),
                  pl.BlockSpec((2 * HEADS, d), lambda *_: (n_main // (2 * HEADS), 0), pipeline_mode=pl.Buffered(1))]
                 + [_resident(a.shape) for a in ins[3:]],
        out_specs=tok_spec,
        out_shape=jax.ShapeDtypeStruct(x.shape, jnp.float32),
        scratch_shapes=[
            pltpu.VMEM((HEADS, QK_DIM, 2 * V_DIM), jnp.float32),
            pltpu.VMEM((SUBLANES, rows_t), jnp.float32),
            pltpu.VMEM((qk_w // LANES, rows_t + SUBLANES, LANES), jnp.float32),
            pltpu.VMEM((rows_t, qk_w), jnp.float32),
            pltpu.VMEM((rows_t, mv_w), jnp.bfloat16),
            pltpu.VMEM((rows_t, mv_w), jnp.float32),
            pltpu.VMEM((rows_t, 2 * mv_w), jnp.bfloat16),
            pltpu.VMEM((d, n_main), jnp.bfloat16),
            pltpu.VMEM((d, LANES), jnp.bfloat16),
            pltpu.VMEM(w_out.shape, jnp.bfloat16),
        ],
        compiler_params=pltpu.CompilerParams(
            dimension_semantics=("arbitrary", "arbitrary"), vmem_limit_bytes=VMEM_LIMIT_BYTES),
        name="mixer",
    )(x, *ins)


def kernel(x, ffn1_norm, ffn1_w_gate, ffn1_w_up, ffn1_w_down, mix_norm, w_in, sgu_ln_g, sgu_ln_b, sgu_w, sgu_b, conv_w, conv_b, igate_b, fgate_b, mh_norm, w_out, ffn2_norm, ffn2_w_gate, ffn2_w_up, ffn2_w_down, final_norm):
    bsz, seq, d = x.shape
    row = lambda a: a.reshape(1, -1)
    fin = row(final_norm)
    for l in range(ffn1_norm.shape[0]):
        x = _ffn(x.reshape(bsz * seq, d), row(ffn1_norm[l]), ffn1_w_gate[l],
                 ffn1_w_up[l], ffn1_w_down[l], fin, False).reshape(bsz, seq, d)
        w_in_t = w_in[l].T
        sgu_bias = jnp.repeat(sgu_b[l].T, LANES, axis=1)
        gate_b = jnp.broadcast_to(
            jnp.concatenate([igate_b[l], fgate_b[l]])[:, None], (2 * HEADS, MIX_ROWS))
        x = _mixer(x, row(mix_norm[l]), w_in_t, w_in_t, row(sgu_ln_g[l]), row(sgu_ln_b[l]), sgu_w[l], sgu_bias,
                   conv_w[l], row(conv_b[l]), gate_b, row(mh_norm[l]), w_out[l])
        last = l == ffn1_norm.shape[0] - 1
        x = _ffn(x.reshape(bsz * seq, d), row(ffn2_norm[l]), ffn2_w_gate[l],
                 ffn2_w_up[l], ffn2_w_down[l], fin, last).reshape(bsz, seq, d)
    return x
```

```python
import functools

import jax
import jax.numpy as jnp
from jax import lax
from jax.experimental import pallas as pl
from jax.experimental.pallas import tpu as pltpu

EPS = 1e-6
LANES = 128
SUBLANES = 8
VMEM_LIMIT_BYTES = 60 * 1024 * 1024

STREAM_CHUNK = 64
SGU_CHUNK = 128
SGU_GROUPS = 4
HEADS = 4
QK_DIM = 64
V_DIM = 128
MIX_CHUNK = 128

FFN_ROWS = 1024
DOWN_ROWS = 512
FFN_COLS = 256
MIX_ROWS = 1024
OUT_CHUNKS = 4
LOG2E = 1.4426950408889634


def _resident(shape):
    zeros = (0,) * len(shape)
    return pl.BlockSpec(shape, lambda *_: zeros, pipeline_mode=pl.Buffered(1))


def _rms(x, gain):
    return x * lax.rsqrt(jnp.mean(x * x, axis=-1, keepdims=True) + EPS) * gain


def _ffn_kernel(x_ref, gain_ref, wg_ref, wu_ref, wd_ref, fin_ref, o_ref, act_ref, *, final_norm):
    blocks = []
    for r in range(0, x_ref.shape[0], DOWN_ROWS // 4):
        xb = x_ref[r:r + DOWN_ROWS // 4, :]
        blocks.append(((xb * gain_ref[...]).astype(jnp.bfloat16),
                       lax.rsqrt(jnp.mean(xb * xb, axis=-1, keepdims=True) + EPS)))
    h = jnp.concatenate([b[0] for b in blocks], axis=0)
    rs = jnp.concatenate([b[1] for b in blocks], axis=0)
    d_ff = wg_ref.shape[1]
    for c in range(d_ff // FFN_COLS):
        cols = slice(c * FFN_COLS, (c + 1) * FFN_COLS)
        g = rs * jnp.dot(h, wg_ref[:, cols].astype(jnp.bfloat16), preferred_element_type=jnp.float32)
        u = rs * jnp.dot(h, wu_ref[:, cols].astype(jnp.bfloat16), preferred_element_type=jnp.float32)
        act_ref[:, cols] = (g * jax.nn.sigmoid(g) * u).astype(jnp.bfloat16)
    for r in range(0, x_ref.shape[0], DOWN_ROWS):
        rows = slice(r, r + DOWN_ROWS)
        y = jnp.dot(act_ref[rows, :], wd_ref[...].astype(jnp.bfloat16), preferred_element_type=jnp.float32)
        out = x_ref[rows, :] + 0.5 * y
        if final_norm:
            out = _rms(out, fin_ref[...])
        o_ref[rows, :] = out


def _ffn(x2d, gain, wg, wu, wd, fin, final_norm):
    n, d = x2d.shape
    d_ff = wg.shape[1]
    assert n % FFN_ROWS == 0 and d_ff % FFN_COLS == 0
    row_spec = pl.BlockSpec((FFN_ROWS, d), lambda i: (i, 0))
    return pl.pallas_call(
        functools.partial(_ffn_kernel, final_norm=final_norm),
        grid=(n // FFN_ROWS,),
        in_specs=[row_spec, _resident((1, d)), _resident((d, d_ff)), _resident((d, d_ff)),
                  _resident((d_ff, d)), _resident((1, d))],
        out_specs=row_spec,
        out_shape=jax.ShapeDtypeStruct((n, d), jnp.float32),
        scratch_shapes=[pltpu.VMEM((FFN_ROWS, d_ff), jnp.bfloat16)],
        compiler_params=pltpu.CompilerParams(
            dimension_semantics=("arbitrary",), vmem_limit_bytes=VMEM_LIMIT_BYTES),
        name="ffn_final" if final_norm else "ffn",
    )(x2d, gain, wg, wu, wd, fin)


def _scan_lanes(x, combine, fill):
    width = x.shape[1]
    lane = lax.broadcasted_iota(jnp.int32, x.shape, 1)
    d = 1
    while d < width:
        x = combine(x, jnp.where(lane >= d, pltpu.roll(x, d, 1), fill))
        d *= 2
    return x


def _mixer_kernel(x_ref, gain_ref, win_ref, wgate_ref, lng_ref, lnb_ref, sw_ref, sb_ref, cw_ref, cb_ref,
                  gb_ref, mh_ref, wout_ref, o_ref,
                  state_ref, mcar_ref, ext_ref, qk_ref, v_ref, og_ref, ymix_ref, win_bf, wgate_bf, wout_bf):
    rows_t = x_ref.shape[0]
    n_chunks = rows_t // MIX_CHUNK
    sgu_w = SGU_GROUPS * LANES
    qk_w = 2 * HEADS * QK_DIM
    mv_w = HEADS * V_DIM
    o_uv, o_qk = 0, 2 * sgu_w
    o_v = o_qk + qk_w
    o_o = o_v + mv_w

    @pl.when((pl.program_id(0) == 0) & (pl.program_id(1) == 0))
    def _():
        win_bf[...] = win_ref[...].T.astype(jnp.bfloat16)
        gate_rows = jnp.concatenate(
            [wgate_ref[...], jnp.zeros((LANES - wgate_ref.shape[0], wgate_ref.shape[1]), jnp.float32)], axis=0)
        wgate_bf[...] = gate_rows.T.astype(jnp.bfloat16)
        wout_bf[...] = wout_ref[...].astype(jnp.bfloat16)

    @pl.when(pl.program_id(1) == 0)
    def _():
        state_ref[...] = jnp.zeros_like(state_ref)
        mcar_ref[...] = jnp.zeros_like(mcar_ref)
        ext_ref[:, 0:SUBLANES, :] = jnp.zeros((qk_w // LANES, SUBLANES, LANES), jnp.float32)

    x = x_ref[...]
    xg = x * gain_ref[...]
    h_raw = xg.astype(jnp.bfloat16)
    rs = lax.rsqrt(jnp.mean(x * x, axis=-1, keepdims=True) + EPS)
    h = (xg * rs).astype(jnp.bfloat16)

    def proj(lo, width):
        return jnp.dot(h, win_bf[:, lo:lo + width], preferred_element_type=jnp.float32)

    z_gate = rs * jnp.dot(h_raw, wgate_bf[...], preferred_element_type=jnp.float32)
    z_qk = rs * jnp.dot(h_raw, win_bf[:, o_qk:o_qk + qk_w], preferred_element_type=jnp.float32)
    for t in range(qk_w // LANES):
        ext_ref[t, SUBLANES:, :] = z_qk[:, t * LANES:(t + 1) * LANES]
    uv = jax.nn.gelu(proj(o_uv, 2 * sgu_w))
    v_ref[...] = proj(o_v, mv_w).astype(jnp.bfloat16)
    og_ref[...] = jax.nn.sigmoid(proj(o_o, mv_w)) * mh_ref[...]

    u = uv[:, :sgu_w]
    v = uv[:, sgu_w:]
    vc = v - jnp.mean(v, axis=-1, keepdims=True)
    vn = vc * lax.rsqrt(jnp.mean(vc * vc, axis=-1, keepdims=True) + EPS) * lng_ref[...] + lnb_ref[...]
    vn = vn.astype(jnp.bfloat16)
    t_blk = lax.broadcasted_iota(jnp.int32, (SGU_CHUNK, SGU_CHUNK), 0) // STREAM_CHUNK
    s_blk = lax.broadcasted_iota(jnp.int32, (SGU_CHUNK, SGU_CHUNK), 1) // STREAM_CHUNK

    def spatial_gating():
        for g in range(SGU_GROUPS):
            cols = slice(g * LANES, (g + 1) * LANES)
            wm = jnp.where(t_blk >= s_blk, sw_ref[g], 0.0).astype(jnp.bfloat16)
            rhs = jnp.concatenate(
                [vn[c * SGU_CHUNK:(c + 1) * SGU_CHUNK, cols] for c in range(n_chunks)], axis=1)
            mixed = jnp.dot(wm, rhs, preferred_element_type=jnp.float32)
            for c in range(n_chunks):
                rows = slice(c * SGU_CHUNK, (c + 1) * SGU_CHUNK)
                gated = u[rows, cols] * (mixed[:, c * LANES:(c + 1) * LANES] + sb_ref[:, cols])
                ymix_ref[rows, cols] = gated.astype(jnp.bfloat16)

    for t in range(qk_w // LANES):
        cols = slice(t * LANES, (t + 1) * LANES)
        conv = cb_ref[:, cols]
        for j in range(cw_ref.shape[0]):
            conv = conv + cw_ref[j:j + 1, cols] * ext_ref[t, pl.ds(SUBLANES - 3 + j, rows_t), :]
        act = conv * jax.nn.sigmoid(conv)
        if t * LANES >= HEADS * QK_DIM:
            act = act * (QK_DIM ** -0.5)
        qk_ref[:, cols] = act
        ext_ref[t, 0:SUBLANES, :] = ext_ref[t, rows_t:rows_t + SUBLANES, :]

    gts = z_gate.T[0:SUBLANES, :] + gb_ref[...]
    logf = pltpu.roll(jax.nn.log_sigmoid(gts), HEADS, 0)
    b_cum = _scan_lanes(logf, jnp.add, 0.0)
    g_in = gts - b_cum
    mm = jnp.maximum(mcar_ref[...], _scan_lanes(g_in, jnp.maximum, -jnp.inf))
    mm_last = jnp.concatenate(
        [jnp.broadcast_to(mm[:, (c + 1) * MIX_CHUNK - 1:(c + 1) * MIX_CHUNK], (SUBLANES, MIX_CHUNK))
         for c in range(n_chunks)], axis=1)
    mm_prev = jnp.concatenate([mcar_ref[:, :MIX_CHUNK], mm_last[:, :rows_t - MIX_CHUNK]], axis=1)
    decay = jnp.exp(mm_prev - mm_last)
    p_last = jnp.exp(g_in - mm_last)
    m_end = b_cum + mm
    mcar_ref[...] = jnp.broadcast_to(m_end[:, rows_t - 1:rows_t], (SUBLANES, rows_t))
    g_in2 = g_in * LOG2E
    mm_prev2 = mm_prev * LOG2E
    cols_t = jnp.concatenate(
        [mm * -LOG2E, m_end * -LOG2E, jnp.zeros((LANES - 2 * SUBLANES, rows_t), jnp.float32)], axis=0).T

    t_idx = lax.broadcasted_iota(jnp.int32, (MIX_CHUNK, MIX_CHUNK), 0)
    s_idx = lax.broadcasted_iota(jnp.int32, (MIX_CHUNK, MIX_CHUNK), 1)
    causal = s_idx <= t_idx
    ones_aug = jnp.ones((MIX_CHUNK, V_DIM), jnp.bfloat16)
    scores, local = {}, {}
    zero_half = jnp.zeros((QK_DIM, MIX_CHUNK), jnp.bfloat16)
    zero_state = jnp.zeros((QK_DIM, 2 * V_DIM), jnp.bfloat16)
    for c in range(n_chunks):
        rows = slice(c * MIX_CHUNK, (c + 1) * MIX_CHUNK)
        for pair in range(HEADS // 2):
            q = qk_ref[rows, pair * LANES:(pair + 1) * LANES]
            q_bf = q.astype(jnp.bfloat16)
            k_t = qk_ref[rows, HEADS * QK_DIM + pair * LANES:HEADS * QK_DIM + (pair + 1) * LANES].T
            k_bf = k_t.astype(jnp.bfloat16)
            k_both = jnp.concatenate(
                [jnp.concatenate([k_bf[:QK_DIM, :], zero_half], axis=0),
                 jnp.concatenate([zero_half, k_bf[QK_DIM:, :]], axis=0)], axis=1)
            s_both = jnp.dot(q_bf, k_both, preferred_element_type=jnp.float32)
            for j in range(2):
                hd = 2 * pair + j
                k_half = k_t[j * QK_DIM:(j + 1) * QK_DIM, :]
                vcols = slice(hd * V_DIM, (hd + 1) * V_DIM)
                v_aug = jnp.concatenate([v_ref[rows, vcols], ones_aug], axis=1)
                scores[c, hd] = s_both[:, j * MIX_CHUNK:(j + 1) * MIX_CHUNK]
                k_w = (k_half * p_last[hd:hd + 1, rows]).astype(jnp.bfloat16)
                local[c, hd] = jnp.dot(k_w, v_aug, preferred_element_type=jnp.float32)
    spatial_gating()
    for c in range(n_chunks):
        rows = slice(c * MIX_CHUNK, (c + 1) * MIX_CHUNK)
        for hd in range(HEADS):
            vcols = slice(hd * V_DIM, (hd + 1) * V_DIM)
            neg_mm = jnp.broadcast_to(cols_t[rows, hd:hd + 1], (MIX_CHUNK, MIX_CHUNK))
            neg_m = jnp.broadcast_to(cols_t[rows, SUBLANES + hd:SUBLANES + hd + 1], (MIX_CHUNK, V_DIM))
            p = jnp.exp2(jnp.where(causal, g_in2[hd:hd + 1, rows] + neg_mm, -jnp.inf))
            w = (p * scores[c, hd]).astype(jnp.bfloat16)
            a_inter = jnp.exp2(mm_prev2[hd:hd + 1, rows] + neg_mm)
            st = state_ref[hd]
            q = qk_ref[rows, (hd // 2) * LANES:(hd // 2 + 1) * LANES]
            lhs = jnp.concatenate([w, (q * a_inter).astype(jnp.bfloat16)], axis=1)
            st_bf = st.astype(jnp.bfloat16)
            st_rows = [st_bf, zero_state] if hd % 2 == 0 else [zero_state, st_bf]
            v_aug = jnp.concatenate([v_ref[rows, vcols], ones_aug], axis=1)
            rhs = jnp.concatenate([v_aug] + st_rows, axis=0)
            nd = jnp.dot(lhs, rhs, preferred_element_type=jnp.float32)
            num = nd[:, :V_DIM]
            den = nd[:, V_DIM:]
            hh = num / jnp.maximum(jnp.abs(den), jnp.exp2(neg_m))
            hh = hh * lax.rsqrt(jnp.mean(hh * hh, axis=-1, keepdims=True) + EPS)
            hh = hh * og_ref[rows, vcols]
            ymix_ref[rows, sgu_w + hd * V_DIM:sgu_w + (hd + 1) * V_DIM] = hh.astype(jnp.bfloat16)
            dec = decay[hd:hd + 1, rows]
            state_ref[hd] = jnp.concatenate([dec, dec], axis=1) * st + local[c, hd]
        if (c + 1) % OUT_CHUNKS == 0:
            done = slice((c + 1 - OUT_CHUNKS) * MIX_CHUNK, (c + 1) * MIX_CHUNK)
            o_ref[done, :] = x_ref[done, :] + jnp.dot(
                ymix_ref[done, :], wout_bf[...], preferred_element_type=jnp.float32)


def _mixer(x, gain, w_in, w_gate, ln_g, ln_b, sgu_w, sgu_bias, conv_w, conv_b, gate_b, mh, w_out):
    bsz, seq, d = x.shape
    rows_t = MIX_ROWS
    assert seq % rows_t == 0 and rows_t % MIX_CHUNK == 0
    tok_spec = pl.BlockSpec((None, rows_t, d), lambda b, s: (b, s, 0))
    ins = (gain, w_in, w_gate, ln_g, ln_b, sgu_w, sgu_bias, conv_w, conv_b, gate_b, mh, w_out)
    n_main = w_in.shape[0] - 2 * HEADS
    qk_w = 2 * HEADS * QK_DIM
    mv_w = HEADS * V_DIM
    return pl.pallas_call(
        _mixer_kernel,
        grid=(bsz, seq // rows_t),
        in_specs=[tok_spec, _resident(gain.shape), _resident((n_main, d)),
                  pl.BlockSpec((2 * HEADS, d), lambda *_: (n_main // (2 * HEADS), 0), pipeline_mode=pl.Buffered(1))]
                 + [_resident(a.shape) for a in ins[3:]],
        out_specs=tok_spec,
        out_shape=jax.ShapeDtypeStruct(x.shape, jnp.float32),
        scratch_shapes=[
            pltpu.VMEM((HEADS, QK_DIM, 2 * V_DIM), jnp.float32),
            pltpu.VMEM((SUBLANES, rows_t), jnp.float32),
            pltpu.VMEM((qk_w // LANES, rows_t + SUBLANES, LANES), jnp.float32),
            pltpu.VMEM((rows_t, qk_w), jnp.float32),
            pltpu.VMEM((rows_t, mv_w), jnp.bfloat16),
            pltpu.VMEM((rows_t, mv_w), jnp.float32),
            pltpu.VMEM((rows_t, 2 * mv_w), jnp.bfloat16),
            pltpu.VMEM((d, n_main), jnp.bfloat16),
            pltpu.VMEM((d, LANES), jnp.bfloat16),
            pltpu.VMEM(w_out.shape, jnp.bfloat16),
        ],
        compiler_params=pltpu.CompilerParams(
            dimension_semantics=("arbitrary", "arbitrary"), vmem_limit_bytes=VMEM_LIMIT_BYTES),
        name="mixer",
    )(x, *ins)


def kernel(x, ffn1_norm, ffn1_w_gate, ffn1_w_up, ffn1_w_down, mix_norm, w_in, sgu_ln_g, sgu_ln_b, sgu_w, sgu_b, conv_w, conv_b, igate_b, fgate_b, mh_norm, w_out, ffn2_norm, ffn2_w_gate, ffn2_w_up, ffn2_w_down, final_norm):
    bsz, seq, d = x.shape
    row = lambda a: a.reshape(1, -1)
    fin = row(final_norm)
    for l in range(ffn1_norm.shape[0]):
        x = _ffn(x.reshape(bsz * seq, d), row(ffn1_norm[l]), ffn1_w_gate[l],
                 ffn1_w_up[l], ffn1_w_down[l], fin, False).reshape(bsz, seq, d)
        w_in_t = w_in[l].T
        sgu_bias = jnp.repeat(sgu_b[l].T, LANES, axis=1)
        gate_b = jnp.broadcast_to(
            jnp.concatenate([igate_b[l], fgate_b[l]])[:, None], (2 * HEADS, MIX_ROWS))
        x = _mixer(x, row(mix_norm[l]), w_in_t, w_in_t, row(sgu_ln_g[l]), row(sgu_ln_b[l]), sgu_w[l], sgu_bias,
                   conv_w[l], row(conv_b[l]), gate_b, row(mh_norm[l]), w_out[l])
        last = l == ffn1_norm.shape[0] - 1
        x = _ffn(x.reshape(bsz * seq, d), row(ffn2_norm[l]), ffn2_w_gate[l],
                 ffn2_w_up[l], ffn2_w_down[l], fin, last).reshape(bsz, seq, d)
    return x
```
